```python
import jax, jax.numpy as jnp
from jax import lax
import numpy as np

D_MODEL = 2048
BATCH = 2
SEQ = 8192
DEPTH = 2

CTX_LEN = 256
GRID_W = 64
N_MIXERS = 2
HEAD_DIM = 128
N_HEADS = D_MODEL // HEAD_DIM
N_KV_HEADS = N_HEADS // 4
GQA_GROUP = N_HEADS // N_KV_HEADS
WINDOW = 128
BLOCK = 128
ROPE_BASE = 10000.0
POOL_WINDOWS = (2, 4, 8, 16)
POOL_GROUP = D_MODEL // len(POOL_WINDOWS)
N_EXPERTS = 64
N_EXPERT_GROUPS = 8
TOPK_GROUPS = 4
TOP_K = 8
EXPERT_FF = D_MODEL // 4
SHARED_FF = EXPERT_FF
ROUTED_SCALE = 2.5
EPS = 1e-6
NEG = -1e30
N_ATTN_LAYERS = (DEPTH + 1) // 2
N_POOL_LAYERS = DEPTH // 2

kernel_name = 'hybrid_swa_sink_pool_moe_dit'


def rmsnorm(x, g):
    x32 = x.astype(jnp.float32)
    y = x32 * lax.rsqrt(jnp.mean(x32 * x32, axis=-1, keepdims=True) + EPS) * g.astype(jnp.float32)
    return y.astype(x.dtype)


def head_rms(x, g):
    x32 = x.astype(jnp.float32)
    y = x32 * lax.rsqrt(jnp.mean(x32 * x32, axis=-1, keepdims=True) + EPS) * g.astype(jnp.float32)
    return y.astype(x.dtype)


def modulate(h, shift, scale):
    return h * (1 + scale) + shift


def adaln(cond, w, b):
    return jnp.split(jax.nn.silu(cond) @ w + b, 6, axis=-1)


def axial_rope(S, dtype):
    rows = S // GRID_W
    row = jnp.repeat(jnp.arange(rows), GRID_W).astype(jnp.float32)
    col = jnp.tile(jnp.arange(GRID_W), rows).astype(jnp.float32)
    nf = HEAD_DIM // 4
    inv = ROPE_BASE ** (-jnp.arange(nf, dtype=jnp.float32) / nf)
    ang = jnp.stack([row[:, None] * inv, col[:, None] * inv], axis=1)
    return jnp.cos(ang).astype(dtype), jnp.sin(ang).astype(dtype)


def apply_axial_rope(x, cos, sin):
    xs = x.reshape(*x.shape[:-1], 2, 2, HEAD_DIM // 4)
    x1, x2 = xs[..., 0, :], xs[..., 1, :]
    c, s = cos[:, None], sin[:, None]
    out = jnp.stack([x1 * c - x2 * s, x2 * c + x1 * s], axis=-2)
    return out.reshape(x.shape)


def band_blocks(t, nb):
    B = t.shape[0]
    tp = jnp.pad(t, ((0, 0), (BLOCK, BLOCK), (0, 0), (0, 0))).reshape(B, nb + 2, BLOCK, N_KV_HEADS, HEAD_DIM)
    return jnp.concatenate([tp[:, :-2], tp[:, 1:-1], tp[:, 2:]], axis=2)


def windowed_sink_attention(h, hc, w_qkv, g_q, g_k, sink, w_o, cos, sin, with_ctx_queries):
    B, S, _ = h.shape
    C = hc.shape[1]
    qd = N_HEADS * HEAD_DIM
    kd = N_KV_HEADS * HEAD_DIM
    scale = HEAD_DIM ** -0.5
    qkv = h @ w_qkv
    q = qkv[..., :qd].reshape(B, S, N_HEADS, HEAD_DIM)
    k = qkv[..., qd:qd + kd].reshape(B, S, N_KV_HEADS, HEAD_DIM)
    v = qkv[..., qd + kd:].reshape(B, S, N_KV_HEADS, HEAD_DIM)
    q = apply_axial_rope(head_rms(q, g_q), cos, sin)
    k = apply_axial_rope(head_rms(k, g_k), cos, sin)
    kv_c = hc @ w_qkv[:, qd:]
    k_c = head_rms(kv_c[..., :kd].reshape(B, C, N_KV_HEADS, HEAD_DIM), g_k)
    v_c = kv_c[..., kd:].reshape(B, C, N_KV_HEADS, HEAD_DIM)
    nb = S // BLOCK
    qb = q.reshape(B, nb, BLOCK, N_KV_HEADS, GQA_GROUP, HEAD_DIM)
    kw = band_blocks(k, nb)
    vw = band_blocks(v, nb)
    s_loc = jnp.einsum('bnqkgd,bnjkd->bnkgqj', qb, kw).astype(jnp.float32) * scale
    qpos = jnp.arange(nb)[:, None, None] * BLOCK + jnp.arange(BLOCK)[None, :, None]
    kpos = (jnp.arange(nb)[:, None, None] - 1) * BLOCK + jnp.arange(3 * BLOCK)[None, None, :]
    valid = (jnp.abs(qpos - kpos) <= WINDOW) & (kpos >= 0) & (kpos < S)
    s_loc = jnp.where(valid[None, :, None, None], s_loc, NEG)
    s_ctx = jnp.einsum('bnqkgd,bckd->bnkgqc', qb, k_c).astype(jnp.float32) * scale
    sink_h = sink.astype(jnp.float32).reshape(N_KV_HEADS, GQA_GROUP)
    sink_l = jnp.broadcast_to(sink_h[None, None, :, :, None, None], s_loc.shape[:-1] + (1,))
    p = jax.nn.softmax(jnp.concatenate([s_loc, s_ctx, sink_l], axis=-1), axis=-1).astype(h.dtype)
    o = (jnp.einsum('bnkgqj,bnjkd->bnqkgd', p[..., :3 * BLOCK], vw)
         + jnp.einsum('bnkgqc,bckd->bnqkgd', p[..., 3 * BLOCK:3 * BLOCK + C], v_c))
    y = o.reshape(B, S, qd) @ w_o
    if with_ctx_queries:
        q_c = head_rms((hc @ w_qkv[:, :qd]).reshape(B, C, N_KV_HEADS, GQA_GROUP, HEAD_DIM), g_q)
        s_cc = jnp.einsum('bckgd,bjkd->bkgcj', q_c, k_c).astype(jnp.float32) * scale
        sink_c = jnp.broadcast_to(sink_h[None, :, :, None, None], s_cc.shape[:-1] + (1,))
        p_c = jax.nn.softmax(jnp.concatenate([s_cc, sink_c], axis=-1), axis=-1).astype(hc.dtype)
        o_c = jnp.einsum('bkgcj,bjkd->bckgd', p_c[..., :C], v_c)
        yc = o_c.reshape(B, C, qd) @ w_o
    else:
        yc = None
    return y, yc


def multiscale_pool(h, w_pool, pool_scale):
    B, S, D = h.shape
    h32 = h.astype(jnp.float32)
    csum = jnp.concatenate([jnp.zeros((B, 1, D), jnp.float32), jnp.cumsum(h32, axis=1)], axis=1)
    t = jnp.arange(S)
    outs = []
    for g, w in enumerate(POOL_WINDOWS):
        lo = jnp.clip(t - w // 2, 0, S)
        hi = jnp.clip(t + (w - w // 2), 0, S)
        cs = csum[..., g * POOL_GROUP:(g + 1) * POOL_GROUP]
        mean = (cs[:, hi] - cs[:, lo]) / (hi - lo).astype(jnp.float32)[:, None]
        outs.append(mean - h32[..., g * POOL_GROUP:(g + 1) * POOL_GROUP])
    d = jnp.stack(outs, axis=2).astype(h.dtype)
    y = jnp.einsum('bsgc,gcd->bsgd', d, w_pool).reshape(B, S, D)
    return y * pool_scale


def moe(h, w_router, b_router, w_gate, w_up, w_down, ws_gate, ws_up, ws_down):
    shape = h.shape
    t = h.reshape(-1, shape[-1])
    n = t.shape[0]
    s = jax.nn.sigmoid(t.astype(jnp.float32) @ w_router.astype(jnp.float32))
    sb = s + b_router.astype(jnp.float32)
    per = N_EXPERTS // N_EXPERT_GROUPS
    gscore = lax.top_k(sb.reshape(n, N_EXPERT_GROUPS, per), 2)[0].sum(-1)
    _, gidx = lax.top_k(gscore, TOPK_GROUPS)
    gmask = jnp.sum(jax.nn.one_hot(gidx, N_EXPERT_GROUPS, dtype=jnp.float32), axis=1) > 0
    emask = jnp.repeat(gmask, per, axis=1)
    _, eidx = lax.top_k(jnp.where(emask, sb, NEG), TOP_K)
    sel = jnp.take_along_axis(s, eidx, axis=1)
    sel = sel / jnp.sum(sel, axis=-1, keepdims=True) * ROUTED_SCALE
    gates = jnp.einsum('nk,nke->en', sel, jax.nn.one_hot(eidx, N_EXPERTS, dtype=jnp.float32)).astype(t.dtype)

    def expert_step(acc, p):
        wg, wu, wd, g = p
        hid = jax.nn.silu(t @ wg) * (t @ wu)
        return acc + (hid @ wd) * g[:, None], None

    routed, _ = lax.scan(expert_step, jnp.zeros_like(t), (w_gate, w_up, w_down, gates))
    shared = (jax.nn.silu(t @ ws_gate) * (t @ ws_up)) @ ws_down
    return (routed + shared).reshape(shape)


def setup_inputs(seed: int = 0) -> dict:
    key = jax.random.key(seed)
    ks = jax.random.split(key, 24)
    f32 = jnp.float32

    def nrm(k, shape, scale):
        return jax.random.normal(k, shape, f32) * scale

    D = D_MODEL
    qkv_w = (N_HEADS + 2 * N_KV_HEADS) * HEAD_DIM
    return {
        'x': nrm(ks[0], (BATCH, SEQ, D), 1.0),
        'c': nrm(ks[1], (BATCH, D), 1.0),
        'ctx': nrm(ks[2], (BATCH, CTX_LEN, D), 1.0),
        'c_ctx': nrm(ks[3], (D,), 1.0),
        'w_ada': nrm(ks[4], (DEPTH, D, 6 * D), 0.5 * D ** -0.5),
        'b_ada': nrm(ks[5], (DEPTH, 6 * D), 0.02),
        'g_norm1': 1.0 + nrm(ks[6], (DEPTH, D), 0.02),
        'g_norm2': 1.0 + nrm(ks[7], (DEPTH, D), 0.02),
        'w_qkv': nrm(ks[8], (N_ATTN_LAYERS, D, qkv_w), D ** -0.5),
        'g_q': 1.0 + nrm(ks[9], (N_ATTN_LAYERS, HEAD_DIM), 0.02),
        'g_k': 1.0 + nrm(ks[10], (N_ATTN_LAYERS, HEAD_DIM), 0.02),
        'sink': nrm(ks[11], (N_ATTN_LAYERS, N_HEADS), 0.5),
        'w_o': nrm(ks[12], (N_ATTN_LAYERS, N_HEADS * HEAD_DIM, D), (N_HEADS * HEAD_DIM) ** -0.5),
        'w_pool': nrm(ks[13], (N_POOL_LAYERS, len(POOL_WINDOWS), POOL_GROUP, POOL_GROUP), POOL_GROUP ** -0.5),
        'pool_scale': 1.0 + nrm(ks[14], (N_POOL_LAYERS, D), 0.02),
        'w_router': nrm(ks[15], (DEPTH, D, N_EXPERTS), D ** -0.5),
        'b_router': nrm(ks[16], (DEPTH, N_EXPERTS), 0.01),
        'w_gate': nrm(ks[17], (DEPTH, N_EXPERTS, D, EXPERT_FF), D ** -0.5),
        'w_up': nrm(ks[18], (DEPTH, N_EXPERTS, D, EXPERT_FF), D ** -0.5),
        'w_down': nrm(ks[19], (DEPTH, N_EXPERTS, EXPERT_FF, D), EXPERT_FF ** -0.5),
        'ws_gate': nrm(ks[20], (DEPTH, D, SHARED_FF), D ** -0.5),
        'ws_up': nrm(ks[21], (DEPTH, D, SHARED_FF), D ** -0.5),
        'ws_down': nrm(ks[22], (DEPTH, SHARED_FF, D), SHARED_FF ** -0.5),
    }


def reference(x, c, ctx, c_ctx, w_ada, b_ada, g_norm1, g_norm2, w_qkv, g_q, g_k, sink, w_o,
              w_pool, pool_scale, w_router, b_router, w_gate, w_up, w_down, ws_gate, ws_up, ws_down):
    S = x.shape[1]
    cos, sin = axial_rope(S, x.dtype)
    last_ctx_reader = max(i for i in range(DEPTH) if i % N_MIXERS == 0)
    for i in range(DEPTH):
        sh1, sc1, ga1, sh2, sc2, ga2 = [m[:, None, :] for m in adaln(c, w_ada[i], b_ada[i])]
        keep_ctx = i < last_ctx_reader
        is_attn = i % N_MIXERS == 0
        h = modulate(rmsnorm(x, g_norm1[i]), sh1, sc1)
        if is_attn or keep_ctx:
            csh1, csc1, cga1, csh2, csc2, cga2 = adaln(c_ctx, w_ada[i], b_ada[i])
            hc = modulate(rmsnorm(ctx, g_norm1[i]), csh1, csc1)
        if is_attn:
            a = i // N_MIXERS
            y, yc = windowed_sink_attention(h, hc, w_qkv[a], g_q[a], g_k[a], sink[a], w_o[a], cos, sin, keep_ctx)
        else:
            p = i // N_MIXERS
            y = multiscale_pool(h, w_pool[p], pool_scale[p])
            yc = multiscale_pool(hc, w_pool[p], pool_scale[p]) if keep_ctx else None
        x = x + ga1 * y
        x = x + ga2 * moe(modulate(rmsnorm(x, g_norm2[i]), sh2, sc2), w_router[i], b_router[i],
                          w_gate[i], w_up[i], w_down[i], ws_gate[i], ws_up[i], ws_down[i])
        if keep_ctx:
            ctx = ctx + cga1 * yc
            ctx = ctx + cga2 * moe(modulate(rmsnorm(ctx, g_norm2[i]), csh2, csc2), w_router[i], b_router[i],
                                   w_gate[i], w_up[i], w_down[i], ws_gate[i], ws_up[i], ws_down[i])
    return x
```

```python
import functools

import jax
import jax.numpy as jnp
from jax import lax
from jax.experimental import pallas as pl
from jax.experimental.pallas import tpu as pltpu

HEAD_DIM = 128
GQA_GROUP = 4
GRID_W = 64
WINDOW = 128
ROPE_BASE = 10000.0
POOL_WINDOWS = (2, 4, 8, 16)
POOL_HALO = 8
N_EXPERT_GROUPS = 8
TOPK_GROUPS = 4
TOP_K = 8
ROUTED_SCALE = 2.5
EPS = 1e-6
NEG = -1e30

LANES = 128
SUBLANES = 8
VMEM_LIMIT = 52 * 1024 * 1024

F32 = jnp.float32
BF16 = jnp.bfloat16


def _cparams(sem):
    return pltpu.CompilerParams(dimension_semantics=sem, vmem_limit_bytes=VMEM_LIMIT)


def _tile(n, want):
    t = min(n, want)
    while n % t:
        t //= 2
    return t


def _silu(v):
    return v * jax.nn.sigmoid(v)


def _rms_mod(x, g, shift, scale):
    ms = jnp.mean(x * x, axis=-1, keepdims=True)
    return (x * lax.rsqrt(ms + EPS) * g) * (1.0 + scale) + shift


def _adaln_kernel(c_ref, w_ref, b_ref, o_ref):
    a = _silu(c_ref[...])
    o_ref[...] = jnp.dot(a.astype(BF16), w_ref[...].astype(BF16),
                         preferred_element_type=F32) + b_ref[...]


def _adaln(cond8, w_ada, b_ada):
    L, D, D6 = w_ada.shape
    tn = _tile(D6, 1024)
    return pl.pallas_call(
        _adaln_kernel,
        out_shape=jax.ShapeDtypeStruct((L, SUBLANES, D6), F32),
        grid=(L, D6 // tn),
        in_specs=[
            pl.BlockSpec((SUBLANES, D), lambda l, j: (0, 0)),
            pl.BlockSpec((None, D, tn), lambda l, j: (l, 0, j)),
            pl.BlockSpec((None, 1, tn), lambda l, j: (l, 0, j)),
        ],
        out_specs=pl.BlockSpec((None, SUBLANES, tn), lambda l, j: (l, 0, j)),
        compiler_params=_cparams(("parallel", "parallel")),
        name="adaln",
    )(cond8, w_ada, b_ada.reshape(L, 1, D6))


def _rope(a, cos, sin_signed):
    lane = lax.broadcasted_iota(jnp.int32, a.shape, 1)
    partner = jnp.where((lane & 32) == 0, pltpu.roll(a, 96, 1), pltpu.roll(a, 32, 1))
    return a * cos + partner * sin_signed


def _qkv_kernel(*refs, n_norm_tiles, rope):
    if rope:
        x_ref, g1_ref, sh_ref, sc_ref, w_ref, gain_ref, cos_ref, sin_ref, o_ref, h_scr = refs
    else:
        x_ref, g1_ref, sh_ref, sc_ref, w_ref, gain_ref, o_ref, h_scr = refs
    j = pl.program_id(1)

    @pl.when(j == 0)
    def _():
        h_scr[...] = _rms_mod(x_ref[...], g1_ref[...], sh_ref[...], sc_ref[...]).astype(BF16)

    acc = jnp.dot(h_scr[...], w_ref[...], preferred_element_type=F32)
    heads = acc.shape[1] // HEAD_DIM

    @pl.when(j < n_norm_tiles)
    def _():
        for hh in range(heads):
            sl = slice(hh * HEAD_DIM, (hh + 1) * HEAD_DIM)
            a = acc[:, sl]
            ms = jnp.mean(a * a, axis=-1, keepdims=True)
            a = a * lax.rsqrt(ms + EPS) * gain_ref[:, sl]
            if rope:
                a = _rope(a, cos_ref[...], sin_ref[...])
            o_ref[:, sl] = a.astype(o_ref.dtype)

    @pl.when(j >= n_norm_tiles)
    def _():
        o_ref[...] = acc.astype(o_ref.dtype)


def _qkv_proj(x2, g1, modv, layer, mod_row_fn, w_bf16, gain, n_norm_cols, rope_tabs, tm):
    N, D = x2.shape
    ncols = w_bf16.shape[1]
    tn = _tile(n_norm_cols, 512)
    assert ncols % tn == 0 and tn % HEAD_DIM == 0
    rope = rope_tabs is not None
    in_specs = [
        pl.BlockSpec((tm, D), lambda i, j: (i, 0)),
        pl.BlockSpec((1, D), lambda i, j: (0, 0)),
        pl.BlockSpec((None, None, None, 1, D), lambda i, j: (layer, 0, mod_row_fn(i), 0, 0)),
        pl.BlockSpec((None, None, None, 1, D), lambda i, j: (layer, 1, mod_row_fn(i), 0, 0)),
        pl.BlockSpec((D, tn), lambda i, j: (0, j)),
        pl.BlockSpec((1, tn), lambda i, j: (0, j)),
    ]
    args = [x2, g1.reshape(1, D), modv, modv, w_bf16, gain]
    if rope:
        cos, sin = rope_tabs
        tpb = cos.shape[0] // tm
        in_specs += [pl.BlockSpec((tm, HEAD_DIM), lambda i, j: (i % tpb, 0))] * 2
        args += [cos, sin]
    return pl.pallas_call(
        functools.partial(_qkv_kernel, n_norm_tiles=n_norm_cols // tn, rope=rope),
        out_shape=jax.ShapeDtypeStruct((N, ncols), BF16),
        grid=(N // tm, ncols // tn),
        in_specs=in_specs,
        out_specs=pl.BlockSpec((tm, tn), lambda i, j: (i, j)),
        scratch_shapes=[pltpu.VMEM((tm, D), BF16)],
        compiler_params=_cparams(("parallel", "arbitrary")),
        name="qkv_rope" if rope else "ctx_kv",
    )(*args)


def _rope_tables(S):
    rows = S // GRID_W
    row = jnp.repeat(jnp.arange(rows), GRID_W).astype(F32)
    col = jnp.tile(jnp.arange(GRID_W), rows).astype(F32)
    nf = HEAD_DIM // 4
    inv = ROPE_BASE ** (-jnp.arange(nf, dtype=F32) / nf)
    ar, ac = row[:, None] * inv, col[:, None] * inv
    cos = jnp.concatenate([jnp.cos(ar), jnp.cos(ar), jnp.cos(ac), jnp.cos(ac)], axis=1)
    sin = jnp.concatenate([-jnp.sin(ar), jnp.sin(ar), -jnp.sin(ac), jnp.sin(ac)], axis=1)
    return cos, sin


def _attn_kernel(sink_ref, q_ref, kp_ref, kc_ref, kn_ref, vp_ref, vc_ref, vn_ref, kx_ref, vx_ref,
                 o_ref, k_scr, v_scr, *, tq, seq):
    i = pl.program_id(1)
    kh = pl.program_id(2)
    nloc = tq + 2 * WINDOW
    nctx = kx_ref.shape[0]
    for scr, prev, cur, nxt, cx in ((k_scr, kp_ref, kc_ref, kn_ref, kx_ref),
                                    (v_scr, vp_ref, vc_ref, vn_ref, vx_ref)):
        scr[0:WINDOW, :] = prev[...]
        scr[WINDOW:WINDOW + tq, :] = cur[...]
        scr[WINDOW + tq:nloc, :] = nxt[...]
        scr[nloc:nloc + nctx, :] = cx[...]
    q0 = i * tq
    shape = (tq, nloc + nctx)
    col = lax.broadcasted_iota(jnp.int32, shape, 1)
    qpos = q0 + lax.broadcasted_iota(jnp.int32, shape, 0)
    kpos = q0 - WINDOW + col
    valid = (col >= nloc) | ((jnp.abs(qpos - kpos) <= WINDOW) & (kpos >= 0) & (kpos < seq))
    kall = k_scr[...]
    vall = v_scr[...]
    for g in range(GQA_GROUP):
        sl = slice(g * HEAD_DIM, (g + 1) * HEAD_DIM)
        s = lax.dot_general(q_ref[:, sl], kall, (((1,), (1,)), ((), ())),
                            preferred_element_type=F32)
        s = jnp.where(valid, s, NEG)
        sink = sink_ref[kh * GQA_GROUP + g]
        m = jnp.maximum(jnp.max(s, axis=-1, keepdims=True), sink)
        p = jnp.exp(s - m)
        denom = jnp.sum(p, axis=-1, keepdims=True) + jnp.exp(sink - m)
        o = jnp.dot(p.astype(BF16), vall, preferred_element_type=F32)
        o_ref[:, sl] = (o / denom).astype(o_ref.dtype)


def _attention(qkv, kvc, sink, B, S, tq):
    N, ncols = qkv.shape
    C = kvc.shape[0] // B
    KV = kvc.shape[1] // (2 * HEAD_DIM)
    H = KV * GQA_GROUP
    assert ncols == (H + 2 * KV) * HEAD_DIM and tq % WINDOW == 0 and S % tq == 0
    tpb, wpb, wpt = S // tq, S // WINDOW, tq // WINDOW
    gw = GQA_GROUP * HEAD_DIM

    def prev_map(col0):
        return lambda b, i, k: (jnp.maximum(b * wpb + i * wpt - 1, b * wpb), col0 + k)

    def cur_map(col0):
        return lambda b, i, k: (b * tpb + i, col0 + k)

    def next_map(col0):
        return lambda b, i, k: (jnp.minimum(b * wpb + (i + 1) * wpt, (b + 1) * wpb - 1), col0 + k)

    halo = lambda m: pl.BlockSpec((WINDOW, HEAD_DIM), m)
    cur = lambda m: pl.BlockSpec((tq, HEAD_DIM), m)
    in_specs = [
        pl.BlockSpec(memory_space=pltpu.SMEM),
        pl.BlockSpec((tq, gw), lambda b, i, k: (b * tpb + i, k)),
        halo(prev_map(H)), cur(cur_map(H)), halo(next_map(H)),
        halo(prev_map(H + KV)), cur(cur_map(H + KV)), halo(next_map(H + KV)),
        pl.BlockSpec((C, HEAD_DIM), lambda b, i, k: (b, k)),
        pl.BlockSpec((C, HEAD_DIM), lambda b, i, k: (b, KV + k)),
    ]
    nkeys = tq + 2 * WINDOW + C
    return pl.pallas_call(
        functools.partial(_attn_kernel, tq=tq, seq=S),
        out_shape=jax.ShapeDtypeStruct((N, H * HEAD_DIM), BF16),
        grid=(B, tpb, KV),
        in_specs=in_specs,
        out_specs=pl.BlockSpec((tq, gw), lambda b, i, k: (b * tpb + i, k)),
        scratch_shapes=[pltpu.VMEM((nkeys, HEAD_DIM), BF16), pltpu.VMEM((nkeys, HEAD_DIM), BF16)],
        compiler_params=_cparams(("parallel", "parallel", "parallel")),
        name="band_attn",
    )(sink, qkv, qkv, qkv, qkv, qkv, qkv, qkv, kvc, kvc)


def _post_mixer(x1, g2_ref, sh2_ref, sc2_ref, wrt_ref, x1_ref, h2_ref, lg_ref):
    x1_ref[...] = x1
    h2 = _rms_mod(x1, g2_ref[...], sh2_ref[...], sc2_ref[...])
    h2_ref[...] = h2
    lg_ref[...] = lax.dot_general(wrt_ref[...], h2, (((1,), (1,)), ((), ())),
                                  precision=lax.Precision.HIGHEST, preferred_element_type=F32)


def _oproj_kernel(o_ref, x_ref, wo_ref, ga1_ref, g2_ref, sh2_ref, sc2_ref, wrt_ref,
                  x1_ref, h2_ref, lg_ref):
    y = jnp.dot(o_ref[...], wo_ref[...], preferred_element_type=F32)
    _post_mixer(x_ref[...] + ga1_ref[...] * y, g2_ref, sh2_ref, sc2_ref, wrt_ref,
                x1_ref, h2_ref, lg_ref)


def _mod_spec(layer, chunk, row_fn, D, nargs):
    if nargs == 1:
        return pl.BlockSpec((None, None, None, 1, D), lambda i: (layer, chunk, row_fn(i), 0, 0))
    return pl.BlockSpec((None, None, None, 1, D), lambda b, i: (layer, chunk, b, 0, 0))


def _post_mixer_out(N, D, E, tm, row_map):
    shapes = (jax.ShapeDtypeStruct((N, D), F32), jax.ShapeDtypeStruct((N, D), F32),
              jax.ShapeDtypeStruct((E, N), F32))
    specs = (pl.BlockSpec((tm, D), lambda *a: (row_map(*a), 0)),
             pl.BlockSpec((tm, D), lambda *a: (row_map(*a), 0)),
             pl.BlockSpec((E, tm), lambda *a: (0, row_map(*a))))
    return shapes, specs


def _oproj(o, x2, wo_bf16, g2, modv, layer, S, wrt, tm):
    N, D = x2.shape
    E = wrt.shape[0]
    qd = o.shape[1]
    tpb = S // tm
    row = lambda i: i // tpb
    shapes, specs = _post_mixer_out(N, D, E, tm, lambda i: i)
    return pl.pallas_call(
        _oproj_kernel,
        out_shape=shapes,
        grid=(N // tm,),
        in_specs=[
            pl.BlockSpec((tm, qd), lambda i: (i, 0)),
            pl.BlockSpec((tm, D), lambda i: (i, 0)),
            pl.BlockSpec((qd, D), lambda i: (0, 0)),
            _mod_spec(layer, 2, row, D, 1),
            pl.BlockSpec((1, D), lambda i: (0, 0)),
            _mod_spec(layer, 3, row, D, 1),
            _mod_spec(layer, 4, row, D, 1),
            pl.BlockSpec((E, D), lambda i: (0, 0)),
        ],
        out_specs=specs,
        compiler_params=_cparams(("parallel",)),
        name="oproj_norm_router",
    )(o, x2, wo_bf16, modv, g2.reshape(1, D), modv, modv, wrt)


def _pool_kernel(xp_ref, xc_ref, xn_ref, g1_ref, sh1_ref, sc1_ref, wp_ref, ps_ref, ga1_ref,
                 g2_ref, sh2_ref, sc2_ref, wrt_ref, x1_ref, h2_ref, lg_ref, halo_scr, *, ts, seq):
    i = pl.program_id(1)
    pos0 = i * ts
    norm = lambda v: _rms_mod(v, g1_ref[...], sh1_ref[...], sc1_ref[...])
    xc = xc_ref[...]
    hc = norm(xc)
    hp = jnp.where(pos0 > 0, norm(xp_ref[...]), 0.0)
    hn = jnp.where(pos0 + ts < seq, norm(xn_ref[...]), 0.0)
    halo_scr[...] = jnp.zeros_like(halo_scr)
    halo_scr[0:POOL_HALO, :] = hp
    halo_scr[POOL_HALO:2 * POOL_HALO, :] = hn
    hc16 = hc.astype(BF16)
    halo16 = halo_scr[...].astype(BF16)

    r_c = lax.broadcasted_iota(jnp.int32, (ts, ts), 0)
    c_c = lax.broadcasted_iota(jnp.int32, (ts, ts), 1)
    r_h = lax.broadcasted_iota(jnp.int32, (ts, LANES), 0)
    c_h = lax.broadcasted_iota(jnp.int32, (ts, LANES), 1)
    p_h = jnp.where(c_h < POOL_HALO, c_h - POOL_HALO, ts + c_h - POOL_HALO)
    in_halo = c_h < 2 * POOL_HALO
    pos = pos0 + lax.broadcasted_iota(jnp.int32, (ts, 1), 0)
    gw = xc.shape[1] // len(POOL_WINDOWS)
    ys = []
    for g, w in enumerate(POOL_WINDOWS):
        lo, hi = w // 2, w - w // 2
        sl = slice(g * gw, (g + 1) * gw)
        band_c = ((c_c >= r_c - lo) & (c_c < r_c + hi)).astype(BF16)
        band_h = (in_halo & (p_h >= r_h - lo) & (p_h < r_h + hi)).astype(BF16)
        tot = (jnp.dot(band_c, hc16[:, sl], preferred_element_type=F32)
               + jnp.dot(band_h, halo16[:, sl], preferred_element_type=F32))
        cnt = (jnp.minimum(pos + hi, seq) - jnp.maximum(pos - lo, 0)).astype(F32)
        d = tot / cnt - hc[:, sl]
        ys.append(jnp.dot(d.astype(BF16), wp_ref[g], preferred_element_type=F32))
    y = jnp.concatenate(ys, axis=1) * ps_ref[...]
    _post_mixer(xc + ga1_ref[...] * y, g2_ref, sh2_ref, sc2_ref, wrt_ref, x1_ref, h2_ref, lg_ref)


def _pool_mixer(x2, g1, g2, modv, layer, B, S, wp_bf16, pool_scale, wrt, ts):
    N, D = x2.shape
    E = wrt.shape[0]
    G, gw, _ = wp_bf16.shape
    tpb = S // ts
    hpt, hpb = ts // POOL_HALO, S // POOL_HALO
    shapes, specs = _post_mixer_out(N, D, E, ts, lambda b, i: b * tpb + i)
    mod = lambda chunk: _mod_spec(layer, chunk, None, D, 2)
    vec = lambda: pl.BlockSpec((1, D), lambda b, i: (0, 0))
    return pl.pallas_call(
        functools.partial(_pool_kernel, ts=ts, seq=S),
        out_shape=shapes,
        grid=(B, tpb),
        in_specs=[
            pl.BlockSpec((POOL_HALO, D), lambda b, i: (jnp.maximum(b * hpb + i * hpt - 1, b * hpb), 0)),
            pl.BlockSpec((ts, D), lambda b, i: (b * tpb + i, 0)),
            pl.BlockSpec((POOL_HALO, D),
                         lambda b, i: (jnp.minimum(b * hpb + (i + 1) * hpt, (b + 1) * hpb - 1), 0)),
            vec(), mod(0), mod(1),
            pl.BlockSpec((G, gw, gw), lambda b, i: (0, 0, 0)),
            vec(), mod(2), vec(), mod(3), mod(4),
            pl.BlockSpec((E, D), lambda b, i: (0, 0)),
        ],
        out_specs=specs,
        scratch_shapes=[pltpu.VMEM((LANES, D), F32)],
        compiler_params=_cparams(("parallel", "parallel")),
        name="pool_norm_router",
    )(x2, x2, x2, g1.reshape(1, D), modv, modv, wp_bf16, pool_scale.reshape(1, D), modv,
      g2.reshape(1, D), modv, modv, wrt)


def _first_argmax(v, idx, big):
    m = jnp.max(v, axis=0, keepdims=True)
    first = jnp.min(jnp.where(v == m, idx, big), axis=0, keepdims=True)
    return m, first


def _route_kernel(lg_ref, b_ref, eidx_ref, gate_ref):
    s = jax.nn.sigmoid(lg_ref[...])
    sb = s + b_ref[...]
    E, tn = s.shape
    per = E // N_EXPERT_GROUPS
    sub = lax.broadcasted_iota(jnp.int32, (per, tn), 0).astype(F32)
    gid = lax.broadcasted_iota(jnp.int32, (N_EXPERT_GROUPS, tn), 0).astype(F32)
    gs = jnp.zeros((N_EXPERT_GROUPS, tn), F32)
    for g in range(N_EXPERT_GROUPS):
        blk = sb[g * per:(g + 1) * per, :]
        m1, i1 = _first_argmax(blk, sub, float(per))
        m2 = jnp.max(jnp.where(sub == i1, -jnp.inf, blk), axis=0, keepdims=True)
        gs = jnp.where(gid == float(g), m1 + m2, gs)
    gsel = jnp.zeros_like(gs)
    for _ in range(TOPK_GROUPS):
        _, gi = _first_argmax(gs, gid, float(N_EXPERT_GROUPS))
        hit = gid == gi
        gsel = jnp.where(hit, 1.0, gsel)
        gs = jnp.where(hit, -jnp.inf, gs)
    cur = jnp.concatenate(
        [jnp.where(gsel[g:g + 1, :] > 0.0, sb[g * per:(g + 1) * per, :], NEG)
         for g in range(N_EXPERT_GROUPS)], axis=0)
    eid = lax.broadcasted_iota(jnp.int32, (E, tn), 0).astype(F32)
    sels = []
    for k in range(TOP_K):
        _, ei = _first_argmax(cur, eid, float(E))
        hit = eid == ei
        eidx_ref[k:k + 1, :] = ei.astype(jnp.int32)
        sels.append(jnp.sum(jnp.where(hit, s, 0.0), axis=0, keepdims=True))
        cur = jnp.where(hit, -jnp.inf, cur)
    denom = sels[0]
    for k in range(1, TOP_K):
        denom = denom + sels[k]
    for k in range(TOP_K):
        gate_ref[k:k + 1, :] = sels[k] / denom * ROUTED_SCALE


def _route(lgt, b_router, tn):
    E, N = lgt.shape
    return pl.pallas_call(
        _route_kernel,
        out_shape=(jax.ShapeDtypeStruct((TOP_K, N), jnp.int32),
                   jax.ShapeDtypeStruct((TOP_K, N), F32)),
        grid=(N // tn,),
        in_specs=[pl.BlockSpec((E, tn), lambda i: (0, i)),
                  pl.BlockSpec((E, 1), lambda i: (0, 0))],
        out_specs=(pl.BlockSpec((TOP_K, tn), lambda i: (0, i)),
                   pl.BlockSpec((TOP_K, tn), lambda i: (0, i))),
        compiler_params=_cparams(("parallel",)),
        name="route_topk",
    )(lgt, b_router.reshape(E, 1))


def _dispatch_plan(eidx, gates, E, tm, n_tiles_max):
    K, N = eidx.shape
    P = N * K
    flat_e = eidx.T.reshape(P)
    flat_g = gates.T.reshape(P)
    perm = jnp.argsort(flat_e, stable=True).astype(jnp.int32)
    counts = jnp.sum((flat_e[:, None] == jnp.arange(E, dtype=jnp.int32)[None, :]).astype(jnp.int32),
                     axis=0)
    tiles_e = (counts + tm - 1) // tm
    tile_end = jnp.cumsum(tiles_e)
    tile_start = tile_end - tiles_e
    n_tiles = tile_end[-1]
    cs = jnp.cumsum(counts) - counts
    tj = jnp.arange(n_tiles_max, dtype=jnp.int32)
    te = jnp.minimum(jnp.sum((tj[:, None] >= tile_end[None, :]).astype(jnp.int32), axis=1), E - 1)
    te = jnp.where(tj < n_tiles, te, te[jnp.maximum(n_tiles - 1, 0)])
    first = (tj - tile_start[te]) * tm
    nvalid = jnp.where(tj < n_tiles, jnp.clip(counts[te] - first, 0, tm), 0)
    r = jnp.arange(tm, dtype=jnp.int32)[None, :]
    valid = r < nvalid[:, None]
    pair = jnp.where(valid, perm[jnp.clip((cs[te] + first)[:, None] + r, 0, P - 1)], 0)
    gate = jnp.where(valid, flat_g[pair], 0.0)
    shape3 = (n_tiles_max, 1, tm)
    i32 = lambda v: v.astype(jnp.int32)
    return (i32(te), i32(n_tiles).reshape(1), i32(nvalid), i32(pair // K).reshape(shape3),
            i32(pair % K).reshape(shape3), gate.reshape(n_tiles_max * tm, 1))


def _moe_kernel(te_ref, nt_ref, nv_ref, tok_ref, tokn_ref, dcol_ref, gate_ref,
                wg_ref, wu_ref, wd_ref, h_hbm, out_hbm,
                xbuf, ybuf, wg_s, wu_s, wd_s, gsem, ssem, *, tm, unroll):
    i = pl.program_id(0)
    last = pl.num_programs(0) - 1
    nt = nt_ref[0]
    slot = i % 2
    D = xbuf.shape[2]

    def gather(tok_smem, s):
        def body(r, carry):
            t = tok_smem[0, 0, r]
            pltpu.make_async_copy(h_hbm.at[pl.ds(t, 1), :], xbuf.at[s, pl.ds(r, 1), :],
                                  gsem.at[s]).start()
            return carry
        lax.fori_loop(0, tm, body, 0, unroll=unroll)

    def scatter_wait(s, nv):
        p = tm
        while p >= 1:
            @pl.when((nv & p) != 0)
            def _(p=p):
                pltpu.make_async_copy(ybuf.at[s, pl.ds(0, p), :],
                                      out_hbm.at[pl.ds(0, p), pl.ds(0, D)], ssem.at[s]).wait()
            p //= 2

    @pl.when(i == 0)
    def _():
        gather(tok_ref, 0)

    @pl.when(i < nt)
    def _():
        @pl.when(i + 1 < nt)
        def _():
            gather(tokn_ref, 1 - slot)

        @pl.when((i == 0) | (te_ref[i] != te_ref[jnp.maximum(i - 1, 0)]))
        def _():
            wg_s[...] = wg_ref[...].astype(BF16)
            wu_s[...] = wu_ref[...].astype(BF16)
            wd_s[...] = wd_ref[...].astype(BF16)

        pltpu.make_async_copy(h_hbm.at[pl.ds(0, tm), :], xbuf.at[slot], gsem.at[slot]).wait()
        x = xbuf[slot].astype(BF16)
        g = jnp.dot(x, wg_s[...], preferred_element_type=F32)
        u = jnp.dot(x, wu_s[...], preferred_element_type=F32)
        hid = (_silu(g) * u).astype(BF16)
        ybuf[slot] = jnp.dot(hid, wd_s[...], preferred_element_type=F32) * gate_ref[...]

        def body(r, carry):
            row = tok_ref[0, 0, r]
            col = pl.multiple_of(dcol_ref[0, 0, r] * D, LANES)
            pltpu.make_async_copy(ybuf.at[slot, pl.ds(r, 1), :],
                                  out_hbm.at[pl.ds(row, 1), pl.ds(col, D)], ssem.at[slot]).start()
            return carry
        nv = nv_ref[i]

        @pl.when(nv == tm)
        def _():
            lax.fori_loop(0, tm, body, 0, unroll=unroll)

        @pl.when(nv < tm)
        def _():
            lax.fori_loop(0, nv, body, 0)

        @pl.when(i >= 1)
        def _():
            scatter_wait(1 - slot, nv_ref[jnp.maximum(i - 1, 0)])

    @pl.when(i == last)
    def _():
        scatter_wait((nt - 1) % 2, nv_ref[nt - 1])


def _experts(h2, plan, w_gate, w_up, w_down, tm):
    te, nt, nv, tok, dcol, gate = plan
    N, D = h2.shape
    E, _, F = w_gate.shape
    T = tok.shape[0]
    assert tm & (tm - 1) == 0 and N >= tm
    smem_blk = lambda m: pl.BlockSpec((1, 1, tm), m, memory_space=pltpu.SMEM)
    cur = lambda i, te, nt, nv: (i, 0, 0)
    wsel = lambda i, te, nt, nv: (te[i], 0, 0)
    grid_spec = pltpu.PrefetchScalarGridSpec(
        num_scalar_prefetch=3,
        grid=(T,),
        in_specs=[
            smem_blk(cur),
            smem_blk(lambda i, te, nt, nv: (jnp.minimum(i + 1, T - 1), 0, 0)),
            smem_blk(cur),
            pl.BlockSpec((tm, 1), lambda i, te, nt, nv: (i, 0)),
            pl.BlockSpec((None, D, F), wsel),
            pl.BlockSpec((None, D, F), wsel),
            pl.BlockSpec((None, F, D), wsel),
            pl.BlockSpec(memory_space=pl.ANY),
        ],
        out_specs=pl.BlockSpec(memory_space=pl.ANY),
        scratch_shapes=[
            pltpu.VMEM((2, tm, D), F32),
            pltpu.VMEM((2, tm, D), F32),
            pltpu.VMEM((D, F), BF16),
            pltpu.VMEM((D, F), BF16),
            pltpu.VMEM((F, D), BF16),
            pltpu.SemaphoreType.DMA((2,)),
            pltpu.SemaphoreType.DMA((2,)),
        ],
    )
    return pl.pallas_call(
        functools.partial(_moe_kernel, tm=tm, unroll=8),
        out_shape=jax.ShapeDtypeStruct((N, TOP_K * D), F32),
        grid_spec=grid_spec,
        compiler_params=_cparams(("arbitrary",)),
        name="moe_experts",
    )(te, nt, nv, tok, tok, dcol, gate, w_gate, w_up, w_down, h2)


def _combine_kernel(h2_ref, x1_ref, po_ref, wsg_ref, wsu_ref, wsd_ref, ga2_ref, o_ref):
    h = h2_ref[...].astype(BF16)
    g = jnp.dot(h, wsg_ref[...], preferred_element_type=F32)
    u = jnp.dot(h, wsu_ref[...], preferred_element_type=F32)
    acc = jnp.dot((_silu(g) * u).astype(BF16), wsd_ref[...], preferred_element_type=F32)
    D = acc.shape[1]
    routed = po_ref[:, 0:D]
    for k in range(1, TOP_K):
        routed = routed + po_ref[:, k * D:(k + 1) * D]
    o_ref[...] = x1_ref[...] + ga2_ref[...] * (routed + acc)


def _combine(h2, x1, pair_out, wsg, wsu, wsd, modv, layer, S, tm):
    N, D = x1.shape
    F = wsg.shape[1]
    tpb = S // tm
    return pl.pallas_call(
        _combine_kernel,
        out_shape=jax.ShapeDtypeStruct((N, D), F32),
        grid=(N // tm,),
        in_specs=[
            pl.BlockSpec((tm, D), lambda i: (i, 0)),
            pl.BlockSpec((tm, D), lambda i: (i, 0)),
            pl.BlockSpec((tm, TOP_K * D), lambda i: (i, 0)),
            pl.BlockSpec((D, F), lambda i: (0, 0)),
            pl.BlockSpec((D, F), lambda i: (0, 0)),
            pl.BlockSpec((F, D), lambda i: (0, 0)),
            _mod_spec(layer, 5, lambda i: i // tpb, D, 1),
        ],
        out_specs=pl.BlockSpec((tm, D), lambda i: (i, 0)),
        compiler_params=_cparams(("parallel",)),
        name="shared_combine",
    )(h2, x1, pair_out, wsg, wsu, wsd, modv)


def _moe_block(x1, h2, lgt, layer, S, modv, b_router, w_gate, w_up, w_down, ws_gate, ws_up, ws_down,
               tiles):
    N, D = x1.shape
    E = lgt.shape[0]
    tm = tiles["expert"]
    eidx, gates = _route(lgt, b_router[layer], _tile(N, tiles["route"]))
    plan = _dispatch_plan(eidx, gates, E, tm, (N * TOP_K) // tm + E)
    pair_out = _experts(h2, plan, w_gate[layer], w_up[layer], w_down[layer], tm)
    return _combine(h2, x1, pair_out, ws_gate[layer].astype(BF16), ws_up[layer].astype(BF16),
                    ws_down[layer].astype(BF16), modv, layer, S, _tile(S, tiles["combine"]))


DEFAULT_TILES = dict(qkv=512, attn=256, oproj=256, pool=256, route=512, expert=256, combine=128)


def kernel(x, c, ctx, c_ctx, w_ada, b_ada, g_norm1, g_norm2, w_qkv, g_q, g_k, sink, w_o, w_pool,
           pool_scale, w_router, b_router, w_gate, w_up, w_down, ws_gate, ws_up, ws_down):
    return _forward(DEFAULT_TILES, x, c, ctx, c_ctx, w_ada, b_ada, g_norm1, g_norm2, w_qkv, g_q, g_k,
                    sink, w_o, w_pool, pool_scale, w_router, b_router, w_gate, w_up, w_down,
                    ws_gate, ws_up, ws_down)


def _forward(tiles, x, c, ctx, c_ctx, w_ada, b_ada, g_norm1, g_norm2, w_qkv, g_q, g_k, sink, w_o,
             w_pool, pool_scale, w_router, b_router, w_gate, w_up, w_down, ws_gate, ws_up, ws_down):
    B, S, D = x.shape
    C = ctx.shape[1]
    L = w_ada.shape[0]
    assert L == 2 and B + 1 <= SUBLANES, "two-layer trunk: attention layer then pooling layer"
    N = B * S
    qd = w_o.shape[1]
    kd = (w_qkv.shape[2] - qd) // 2

    cond8 = jnp.zeros((SUBLANES, D), F32).at[:B].set(c).at[B].set(c_ctx)
    mod = _adaln(cond8, w_ada, b_ada)
    modv = mod.reshape(L, SUBLANES, 6, D).transpose(0, 2, 1, 3).reshape(L, 6, SUBLANES, 1, D)
    wrt = jnp.swapaxes(w_router, 1, 2)
    x2 = x.reshape(N, D)

    scale = HEAD_DIM ** -0.5
    wq = w_qkv[0].astype(BF16)
    ones_v = jnp.ones((kd,), F32)
    gain_lat = jnp.concatenate([jnp.tile(g_q[0] * scale, qd // HEAD_DIM),
                                jnp.tile(g_k[0], kd // HEAD_DIM), ones_v]).reshape(1, -1)
    gain_ctx = jnp.concatenate([jnp.tile(g_k[0], kd // HEAD_DIM), ones_v]).reshape(1, -1)
    tmq = _tile(S, tiles["qkv"])
    qkv = _qkv_proj(x2, g_norm1[0], modv, 0, lambda i: i // (S // tmq), wq, gain_lat, qd + kd,
                    _rope_tables(S), tmq)
    kvc = _qkv_proj(ctx.reshape(B * C, D), g_norm1[0], modv, 0, lambda i: B, wq[:, qd:], gain_ctx,
                    kd, None, C)
    o = _attention(qkv, kvc, sink[0], B, S, _tile(S, tiles["attn"]))
    x1, h2, lgt = _oproj(o, x2, w_o[0].astype(BF16), g_norm2[0], modv, 0, S, wrt[0],
                         _tile(S, tiles["oproj"]))
    moe_w = (modv, b_router, w_gate, w_up, w_down, ws_gate, ws_up, ws_down, tiles)
    x2 = _moe_block(x1, h2, lgt, 0, S, *moe_w)

    x1, h2, lgt = _pool_mixer(x2, g_norm1[1], g_norm2[1], modv, 1, B, S, w_pool[0].astype(BF16),
                              pool_scale[0], wrt[1], _tile(S, tiles["pool"]))
    x2 = _moe_block(x1, h2, lgt, 1, S, *moe_w)
    return x2.reshape(B, S, D)
```

```python
import functools

import jax
import jax.numpy as jnp
from jax import lax
from jax.experimental import pallas as pl
from jax.experimental.pallas import tpu as pltpu

HEAD_DIM = 128
GQA_GROUP = 4
GRID_W = 64
WINDOW = 128
ROPE_BASE = 10000.0
POOL_WINDOWS = (2, 4, 8, 16)
POOL_HALO = 8
N_EXPERT_GROUPS = 8
TOPK_GROUPS = 4
TOP_K = 8
ROUTED_SCALE = 2.5
EPS = 1e-6
NEG = -1e30

LANES = 128
SUBLANES = 8
VMEM_LIMIT = 52 * 1024 * 1024

F32 = jnp.float32
BF16 = jnp.bfloat16


def _cparams(sem):
    return pltpu.CompilerParams(dimension_semantics=sem, vmem_limit_bytes=VMEM_LIMIT)


def _tile(n, want):
    t = min(n, want)
    while n % t:
        t //= 2
    return t


def _silu(v):
    return v * jax.nn.sigmoid(v)


def _rms_mod(x, g, shift, scale):
    ms = jnp.mean(x * x, axis=-1, keepdims=True)
    return (x * lax.rsqrt(ms + EPS) * g) * (1.0 + scale) + shift


def _adaln_kernel(c_ref, w_ref, b_ref, o_ref):
    a = _silu(c_ref[...])
    o_ref[...] = jnp.dot(a.astype(BF16), w_ref[...].astype(BF16),
                         preferred_element_type=F32) + b_ref[...]


def _adaln(cond8, w_ada, b_ada):
    L, D, D6 = w_ada.shape
    tn = _tile(D6, 1024)
    return pl.pallas_call(
        _adaln_kernel,
        out_shape=jax.ShapeDtypeStruct((L, SUBLANES, D6), F32),
        grid=(L, D6 // tn),
        in_specs=[
            pl.BlockSpec((SUBLANES, D), lambda l, j: (0, 0)),
            pl.BlockSpec((None, D, tn), lambda l, j: (l, 0, j)),
            pl.BlockSpec((None, 1, tn), lambda l, j: (l, 0, j)),
        ],
        out_specs=pl.BlockSpec((None, SUBLANES, tn), lambda l, j: (l, 0, j)),
        compiler_params=_cparams(("parallel", "parallel")),
        name="adaln",
    )(cond8, w_ada, b_ada.reshape(L, 1, D6))


def _rope(a, cos, sin_signed):
    lane = lax.broadcasted_iota(jnp.int32, a.shape, 1)
    partner = jnp.where((lane & 32) == 0, pltpu.roll(a, 96, 1), pltpu.roll(a, 32, 1))
    return a * cos + partner * sin_signed


def _qkv_kernel(*refs, n_norm_tiles, rope):
    if rope:
        x_ref, g1_ref, sh_ref, sc_ref, w_ref, gain_ref, cos_ref, sin_ref, o_ref, h_scr = refs
    else:
        x_ref, g1_ref, sh_ref, sc_ref, w_ref, gain_ref, o_ref, h_scr = refs
    j = pl.program_id(1)

    @pl.when(j == 0)
    def _():
        h_scr[...] = _rms_mod(x_ref[...], g1_ref[...], sh_ref[...], sc_ref[...]).astype(BF16)

    acc = jnp.dot(h_scr[...], w_ref[...], preferred_element_type=F32)
    is_norm = j < n_norm_tiles
    for hh in range(acc.shape[1] // HEAD_DIM):
        sl = slice(hh * HEAD_DIM, (hh + 1) * HEAD_DIM)
        a = acc[:, sl]
        ms = jnp.mean(a * a, axis=-1, keepdims=True)
        n = a * lax.rsqrt(ms + EPS) * gain_ref[:, sl]
        if rope:
            n = _rope(n, cos_ref[...], sin_ref[...])
        o_ref[:, sl] = jnp.where(is_norm, n, a).astype(o_ref.dtype)


def _qkv_proj(x2, g1, modv, layer, mod_row_fn, w_bf16, gain, n_norm_cols, rope_tabs, tm):
    N, D = x2.shape
    ncols = w_bf16.shape[1]
    tn = _tile(n_norm_cols, 512)
    assert ncols % tn == 0 and tn % HEAD_DIM == 0
    rope = rope_tabs is not None
    in_specs = [
        pl.BlockSpec((tm, D), lambda i, j: (i, 0)),
        pl.BlockSpec((1, D), lambda i, j: (0, 0)),
        pl.BlockSpec((None, None, None, 1, D), lambda i, j: (layer, 0, mod_row_fn(i), 0, 0)),
        pl.BlockSpec((None, None, None, 1, D), lambda i, j: (layer, 1, mod_row_fn(i), 0, 0)),
        pl.BlockSpec((D, tn), lambda i, j: (0, j)),
        pl.BlockSpec((1, tn), lambda i, j: (0, j)),
    ]
    args = [x2, g1.reshape(1, D), modv, modv, w_bf16, gain]
    if rope:
        cos, sin = rope_tabs
        tpb = cos.shape[0] // tm
        in_specs += [pl.BlockSpec((tm, HEAD_DIM), lambda i, j: (i % tpb, 0))] * 2
        args += [cos, sin]
    return pl.pallas_call(
        functools.partial(_qkv_kernel, n_norm_tiles=n_norm_cols // tn, rope=rope),
        out_shape=jax.ShapeDtypeStruct((N, ncols), BF16),
        grid=(N // tm, ncols // tn),
        in_specs=in_specs,
        out_specs=pl.BlockSpec((tm, tn), lambda i, j: (i, j)),
        scratch_shapes=[pltpu.VMEM((tm, D), BF16)],
        compiler_params=_cparams(("parallel", "arbitrary")),
        name="qkv_rope" if rope else "ctx_kv",
    )(*args)


def _rope_tables(S):
    rows = S // GRID_W
    row = jnp.repeat(jnp.arange(rows), GRID_W).astype(F32)
    col = jnp.tile(jnp.arange(GRID_W), rows).astype(F32)
    nf = HEAD_DIM // 4
    inv = ROPE_BASE ** (-jnp.arange(nf, dtype=F32) / nf)
    ar, ac = row[:, None] * inv, col[:, None] * inv
    cos = jnp.concatenate([jnp.cos(ar), jnp.cos(ar), jnp.cos(ac), jnp.cos(ac)], axis=1)
    sin = jnp.concatenate([-jnp.sin(ar), jnp.sin(ar), -jnp.sin(ac), jnp.sin(ac)], axis=1)
    return cos, sin


def _attn_kernel(sink_ref, q_ref, kp_ref, kc_ref, kn_ref, vp_ref, vc_ref, vn_ref, kx_ref, vx_ref,
                 o_ref, k_scr, v_scr, *, tq, seq):
    i = pl.program_id(1)
    kh = pl.program_id(2)
    nloc = tq + 2 * WINDOW
    nctx = kx_ref.shape[0]
    for scr, prev, cur, nxt, cx in ((k_scr, kp_ref, kc_ref, kn_ref, kx_ref),
                                    (v_scr, vp_ref, vc_ref, vn_ref, vx_ref)):
        scr[0:WINDOW, :] = prev[...]
        scr[WINDOW:WINDOW + tq, :] = cur[...]
        scr[WINDOW + tq:nloc, :] = nxt[...]
        scr[nloc:nloc + nctx, :] = cx[...]
    q0 = i * tq
    shape = (tq, nloc + nctx)
    col = lax.broadcasted_iota(jnp.int32, shape, 1)
    qpos = q0 + lax.broadcasted_iota(jnp.int32, shape, 0)
    kpos = q0 - WINDOW + col
    valid = (col >= nloc) | ((jnp.abs(qpos - kpos) <= WINDOW) & (kpos >= 0) & (kpos < seq))
    kall = k_scr[...]
    vall = v_scr[...]
    for g in range(GQA_GROUP):
        sl = slice(g * HEAD_DIM, (g + 1) * HEAD_DIM)
        s = lax.dot_general(q_ref[:, sl], kall, (((1,), (1,)), ((), ())),
                            preferred_element_type=F32)
        s = jnp.where(valid, s, NEG)
        sink = sink_ref[kh * GQA_GROUP + g]
        m = jnp.maximum(jnp.max(s, axis=-1, keepdims=True), sink)
        p = jnp.exp(s - m)
        denom = jnp.sum(p, axis=-1, keepdims=True) + jnp.exp(sink - m)
        o = jnp.dot(p.astype(BF16), vall, preferred_element_type=F32)
        o_ref[:, sl] = (o / denom).astype(o_ref.dtype)


def _attention(qkv, kvc, sink, B, S, tq):
    N, ncols = qkv.shape
    C = kvc.shape[0] // B
    KV = kvc.shape[1] // (2 * HEAD_DIM)
    H = KV * GQA_GROUP
    assert ncols == (H + 2 * KV) * HEAD_DIM and tq % WINDOW == 0 and S % tq == 0
    tpb, wpb, wpt = S // tq, S // WINDOW, tq // WINDOW
    gw = GQA_GROUP * HEAD_DIM

    def prev_map(col0):
        return lambda b, i, k: (jnp.maximum(b * wpb + i * wpt - 1, b * wpb), col0 + k)

    def cur_map(col0):
        return lambda b, i, k: (b * tpb + i, col0 + k)

    def next_map(col0):
        return lambda b, i, k: (jnp.minimum(b * wpb + (i + 1) * wpt, (b + 1) * wpb - 1), col0 + k)

    halo = lambda m: pl.BlockSpec((WINDOW, HEAD_DIM), m)
    cur = lambda m: pl.BlockSpec((tq, HEAD_DIM), m)
    in_specs = [
        pl.BlockSpec(memory_space=pltpu.SMEM),
        pl.BlockSpec((tq, gw), lambda b, i, k: (b * tpb + i, k)),
        halo(prev_map(H)), cur(cur_map(H)), halo(next_map(H)),
        halo(prev_map(H + KV)), cur(cur_map(H + KV)), halo(next_map(H + KV)),
        pl.BlockSpec((C, HEAD_DIM), lambda b, i, k: (b, k)),
        pl.BlockSpec((C, HEAD_DIM), lambda b, i, k: (b, KV + k)),
    ]
    nkeys = tq + 2 * WINDOW + C
    return pl.pallas_call(
        functools.partial(_attn_kernel, tq=tq, seq=S),
        out_shape=jax.ShapeDtypeStruct((N, H * HEAD_DIM), BF16),
        grid=(B, tpb, KV),
        in_specs=in_specs,
        out_specs=pl.BlockSpec((tq, gw), lambda b, i, k: (b * tpb + i, k)),
        scratch_shapes=[pltpu.VMEM((nkeys, HEAD_DIM), BF16), pltpu.VMEM((nkeys, HEAD_DIM), BF16)],
        compiler_params=_cparams(("parallel", "parallel", "parallel")),
        name="band_attn",
    )(sink, qkv, qkv, qkv, qkv, qkv, qkv, qkv, kvc, kvc)


def _split_bf16(v):
    hi = v.astype(BF16)
    return hi, (v - hi.astype(F32)).astype(BF16)


def _post_mixer(x1, g2_ref, sh2_ref, sc2_ref, wr_ref, x1_ref, h2_ref, lg_ref):
    x1_ref[...] = x1
    h2 = _rms_mod(x1, g2_ref[...], sh2_ref[...], sc2_ref[...])
    h2_ref[...] = h2
    hi, lo = _split_bf16(h2)
    dot = functools.partial(jnp.dot, preferred_element_type=F32)
    lg_ref[...] = dot(hi, wr_ref[0]) + (dot(hi, wr_ref[1]) + dot(lo, wr_ref[0]))


def _oproj_kernel(o_ref, x_ref, wo_ref, ga1_ref, g2_ref, sh2_ref, sc2_ref, wrt_ref,
                  x1_ref, h2_ref, lg_ref):
    y = jnp.dot(o_ref[...], wo_ref[...], preferred_element_type=F32)
    _post_mixer(x_ref[...] + ga1_ref[...] * y, g2_ref, sh2_ref, sc2_ref, wrt_ref,
                x1_ref, h2_ref, lg_ref)


def _mod_spec(layer, chunk, row_fn, D, nargs):
    if nargs == 1:
        return pl.BlockSpec((None, None, None, 1, D), lambda i: (layer, chunk, row_fn(i), 0, 0))
    return pl.BlockSpec((None, None, None, 1, D), lambda b, i: (layer, chunk, b, 0, 0))


def _post_mixer_out(N, D, tm, row_map):
    shapes = (jax.ShapeDtypeStruct((N, D), F32), jax.ShapeDtypeStruct((N, D), F32),
              jax.ShapeDtypeStruct((N, LANES), F32))
    specs = (pl.BlockSpec((tm, D), lambda *a: (row_map(*a), 0)),
             pl.BlockSpec((tm, D), lambda *a: (row_map(*a), 0)),
             pl.BlockSpec((tm, LANES), lambda *a: (row_map(*a), 0)))
    return shapes, specs


def _router_weights(w_router):
    L, D, E = w_router.shape
    assert E <= LANES
    w = jnp.pad(w_router, ((0, 0), (0, 0), (0, LANES - E)))
    hi = w.astype(BF16)
    lo = (w - hi.astype(F32)).astype(BF16)
    return jnp.stack([hi, lo], axis=1)


def _oproj(o, x2, wo_bf16, g2, modv, layer, S, wr, tm):
    N, D = x2.shape
    qd = o.shape[1]
    tpb = S // tm
    row = lambda i: i // tpb
    shapes, specs = _post_mixer_out(N, D, tm, lambda i: i)
    return pl.pallas_call(
        _oproj_kernel,
        out_shape=shapes,
        grid=(N // tm,),
        in_specs=[
            pl.BlockSpec((tm, qd), lambda i: (i, 0)),
            pl.BlockSpec((tm, D), lambda i: (i, 0)),
            pl.BlockSpec((qd, D), lambda i: (0, 0)),
            _mod_spec(layer, 2, row, D, 1),
            pl.BlockSpec((1, D), lambda i: (0, 0)),
            _mod_spec(layer, 3, row, D, 1),
            _mod_spec(layer, 4, row, D, 1),
            pl.BlockSpec((None, 2, D, LANES), lambda i: (layer, 0, 0, 0)),
        ],
        out_specs=specs,
        compiler_params=_cparams(("parallel",)),
        name="oproj_norm_router",
    )(o, x2, wo_bf16, modv, g2.reshape(1, D), modv, modv, wr)


def _pool_kernel(xp_ref, xc_ref, xn_ref, g1_ref, sh1_ref, sc1_ref, wp_ref, ps_ref, ga1_ref,
                 g2_ref, sh2_ref, sc2_ref, wrt_ref, x1_ref, h2_ref, lg_ref, halo_scr, *, ts, seq):
    i = pl.program_id(1)
    pos0 = i * ts
    norm = lambda v: _rms_mod(v, g1_ref[...], sh1_ref[...], sc1_ref[...])
    xc = xc_ref[...]
    hc = norm(xc)
    hp = jnp.where(pos0 > 0, norm(xp_ref[...]), 0.0)
    hn = jnp.where(pos0 + ts < seq, norm(xn_ref[...]), 0.0)
    halo_scr[...] = jnp.zeros_like(halo_scr)
    halo_scr[0:POOL_HALO, :] = hp
    halo_scr[POOL_HALO:2 * POOL_HALO, :] = hn
    hc16 = hc.astype(BF16)
    halo16 = halo_scr[...].astype(BF16)

    r_c = lax.broadcasted_iota(jnp.int32, (ts, ts), 0)
    c_c = lax.broadcasted_iota(jnp.int32, (ts, ts), 1)
    r_h = lax.broadcasted_iota(jnp.int32, (ts, LANES), 0)
    c_h = lax.broadcasted_iota(jnp.int32, (ts, LANES), 1)
    p_h = jnp.where(c_h < POOL_HALO, c_h - POOL_HALO, ts + c_h - POOL_HALO)
    in_halo = c_h < 2 * POOL_HALO
    pos = pos0 + lax.broadcasted_iota(jnp.int32, (ts, 1), 0)
    gw = xc.shape[1] // len(POOL_WINDOWS)
    ys = []
    for g, w in enumerate(POOL_WINDOWS):
        lo, hi = w // 2, w - w // 2
        sl = slice(g * gw, (g + 1) * gw)
        band_c = ((c_c >= r_c - lo) & (c_c < r_c + hi)).astype(BF16)
        band_h = (in_halo & (p_h >= r_h - lo) & (p_h < r_h + hi)).astype(BF16)
        tot = (jnp.dot(band_c, hc16[:, sl], preferred_element_type=F32)
               + jnp.dot(band_h, halo16[:, sl], preferred_element_type=F32))
        cnt = (jnp.minimum(pos + hi, seq) - jnp.maximum(pos - lo, 0)).astype(F32)
        d = tot / cnt - hc[:, sl]
        ys.append(jnp.dot(d.astype(BF16), wp_ref[g], preferred_element_type=F32))
    y = jnp.concatenate(ys, axis=1) * ps_ref[...]
    _post_mixer(xc + ga1_ref[...] * y, g2_ref, sh2_ref, sc2_ref, wrt_ref, x1_ref, h2_ref, lg_ref)


def _pool_mixer(x2, g1, g2, modv, layer, B, S, wp_bf16, pool_scale, wr, ts):
    N, D = x2.shape
    G, gw, _ = wp_bf16.shape
    tpb = S // ts
    hpt, hpb = ts // POOL_HALO, S // POOL_HALO
    shapes, specs = _post_mixer_out(N, D, ts, lambda b, i: b * tpb + i)
    mod = lambda chunk: _mod_spec(layer, chunk, None, D, 2)
    vec = lambda: pl.BlockSpec((1, D), lambda b, i: (0, 0))
    return pl.pallas_call(
        functools.partial(_pool_kernel, ts=ts, seq=S),
        out_shape=shapes,
        grid=(B, tpb),
        in_specs=[
            pl.BlockSpec((POOL_HALO, D), lambda b, i: (jnp.maximum(b * hpb + i * hpt - 1, b * hpb), 0)),
            pl.BlockSpec((ts, D), lambda b, i: (b * tpb + i, 0)),
            pl.BlockSpec((POOL_HALO, D),
                         lambda b, i: (jnp.minimum(b * hpb + (i + 1) * hpt, (b + 1) * hpb - 1), 0)),
            vec(), mod(0), mod(1),
            pl.BlockSpec((G, gw, gw), lambda b, i: (0, 0, 0)),
            vec(), mod(2), vec(), mod(3), mod(4),
            pl.BlockSpec((None, 2, D, LANES), lambda b, i: (layer, 0, 0, 0)),
        ],
        out_specs=specs,
        scratch_shapes=[pltpu.VMEM((LANES, D), F32)],
        compiler_params=_cparams(("parallel", "parallel")),
        name="pool_norm_router",
    )(x2, x2, x2, g1.reshape(1, D), modv, modv, wp_bf16, pool_scale.reshape(1, D), modv,
      g2.reshape(1, D), modv, modv, wr)


def _first_argmax(v, idx, big):
    m = jnp.max(v, axis=0, keepdims=True)
    first = jnp.min(jnp.where(v == m, idx, big), axis=0, keepdims=True)
    return m, first


def _route_kernel(lg_ref, b_ref, eidx_ref, gate_ref):
    E = b_ref.shape[0]
    s = jax.nn.sigmoid(lg_ref[...].T[0:E, :])
    sb = s + b_ref[...]
    tn = s.shape[1]
    per = E // N_EXPERT_GROUPS
    sub = lax.broadcasted_iota(jnp.int32, (per, tn), 0).astype(F32)
    gid = lax.broadcasted_iota(jnp.int32, (N_EXPERT_GROUPS, tn), 0).astype(F32)
    gs = jnp.zeros((N_EXPERT_GROUPS, tn), F32)
    for g in range(N_EXPERT_GROUPS):
        blk = sb[g * per:(g + 1) * per, :]
        m1, i1 = _first_argmax(blk, sub, float(per))
        m2 = jnp.max(jnp.where(sub == i1, -jnp.inf, blk), axis=0, keepdims=True)
        gs = jnp.where(gid == float(g), m1 + m2, gs)
    gsel = jnp.zeros_like(gs)
    for _ in range(TOPK_GROUPS):
        _, gi = _first_argmax(gs, gid, float(N_EXPERT_GROUPS))
        hit = gid == gi
        gsel = jnp.where(hit, 1.0, gsel)
        gs = jnp.where(hit, -jnp.inf, gs)
    cur = jnp.concatenate(
        [jnp.where(gsel[g:g + 1, :] > 0.0, sb[g * per:(g + 1) * per, :], NEG)
         for g in range(N_EXPERT_GROUPS)], axis=0)
    eid = lax.broadcasted_iota(jnp.int32, (E, tn), 0).astype(F32)
    sels = []
    for k in range(TOP_K):
        _, ei = _first_argmax(cur, eid, float(E))
        hit = eid == ei
        eidx_ref[k:k + 1, :] = ei.astype(jnp.int32)
        sels.append(jnp.sum(jnp.where(hit, s, 0.0), axis=0, keepdims=True))
        cur = jnp.where(hit, -jnp.inf, cur)
    denom = sels[0]
    for k in range(1, TOP_K):
        denom = denom + sels[k]
    for k in range(TOP_K):
        gate_ref[k:k + 1, :] = sels[k] / denom * ROUTED_SCALE


def _route(lg, b_router, tn):
    N = lg.shape[0]
    E = b_router.shape[0]
    return pl.pallas_call(
        _route_kernel,
        out_shape=(jax.ShapeDtypeStruct((TOP_K, N), jnp.int32),
                   jax.ShapeDtypeStruct((TOP_K, N), F32)),
        grid=(N // tn,),
        in_specs=[pl.BlockSpec((tn, LANES), lambda i: (i, 0)),
                  pl.BlockSpec((E, 1), lambda i: (0, 0))],
        out_specs=(pl.BlockSpec((TOP_K, tn), lambda i: (0, i)),
                   pl.BlockSpec((TOP_K, tn), lambda i: (0, i))),
        compiler_params=_cparams(("parallel",)),
        name="route_topk",
    )(lg, b_router.reshape(E, 1))


def _dispatch_plan(eidx, gates, E, tm, n_tiles_max):
    K, N = eidx.shape
    P = N * K
    flat_e = eidx.T.reshape(P)
    flat_g = gates.T.reshape(P)
    perm = jnp.argsort(flat_e, stable=True).astype(jnp.int32)
    counts = jnp.sum((flat_e[:, None] == jnp.arange(E, dtype=jnp.int32)[None, :]).astype(jnp.int32),
                     axis=0)
    tiles_e = (counts + tm - 1) // tm
    tile_end = jnp.cumsum(tiles_e)
    tile_start = tile_end - tiles_e
    n_tiles = tile_end[-1]
    cs = jnp.cumsum(counts) - counts
    tj = jnp.arange(n_tiles_max, dtype=jnp.int32)
    te = jnp.minimum(jnp.sum((tj[:, None] >= tile_end[None, :]).astype(jnp.int32), axis=1), E - 1)
    te = jnp.where(tj < n_tiles, te, te[jnp.maximum(n_tiles - 1, 0)])
    first = (tj - tile_start[te]) * tm
    nvalid = jnp.where(tj < n_tiles, jnp.clip(counts[te] - first, 0, tm), 0)
    r = jnp.arange(tm, dtype=jnp.int32)[None, :]
    valid = r < nvalid[:, None]
    pair = jnp.where(valid, perm[jnp.clip((cs[te] + first)[:, None] + r, 0, P - 1)], 0)
    gate = jnp.where(valid, flat_g[pair], 0.0)
    tok, k = pair // K, pair % K
    grp, row = tok // SUBLANES, tok % SUBLANES
    i32 = lambda v: v.astype(jnp.int32)
    plan3 = lambda a: i32(jnp.concatenate([a, row], axis=1)).reshape(n_tiles_max, 1, 2 * tm)
    return (i32(te), i32(n_tiles).reshape(1), i32(nvalid), plan3(grp), plan3(grp * K + k),
            gate.reshape(n_tiles_max * tm, 1))


def _moe_kernel(te_ref, nt_ref, nv_ref, src0_ref, srcn_ref, dstp_ref, gate_ref,
                wg_ref, wu_ref, wd_ref, h_hbm, out_hbm,
                xbuf, ybuf, wg_s, wu_s, wd_s, gsem, ssem, *, tm):
    i = pl.program_id(0)
    nt = nt_ref[0]
    slot = i % 2
    other = 1 - slot
    D = xbuf.shape[3]
    nvp = nv_ref[jnp.maximum(i - 1, 0)]

    def hbm_row(ref, idx_smem, r):
        return ref.at[idx_smem[0, 0, r], pl.ds(idx_smem[0, 0, tm + r], 1), :]

    def vmem_row(buf, s, r):
        return buf.at[s, r // SUBLANES, pl.ds(r % SUBLANES, 1), :]

    def gather_issue(idx_smem, s):
        for r in range(tm):
            pltpu.make_async_copy(hbm_row(h_hbm, idx_smem, r), vmem_row(xbuf, s, r),
                                  gsem.at[s]).start()

    def gather_wait(s):
        pltpu.make_async_copy(h_hbm.at[pl.ds(0, tm // SUBLANES)], xbuf.at[s], gsem.at[s]).wait()

    def scatter_issue_full(s):
        for r in range(tm):
            pltpu.make_async_copy(vmem_row(ybuf, s, r), hbm_row(out_hbm, dstp_ref, r),
                                  ssem.at[s]).start()

    def scatter_wait(s, nv):
        p = tm
        while p >= 1:
            @pl.when((nv & p) != 0)
            def _(p=p):
                if p >= SUBLANES:
                    src, dst = ybuf.at[s, pl.ds(0, p // SUBLANES)], out_hbm.at[pl.ds(0, p // SUBLANES)]
                else:
                    src, dst = ybuf.at[s, 0, pl.ds(0, p), :], out_hbm.at[0, pl.ds(0, p), :]
                pltpu.make_async_copy(src, dst, ssem.at[s]).wait()
            p //= 2

    @pl.when(i == 0)
    def _():
        gather_issue(src0_ref, 0)

    @pl.when((i >= 1) & (i <= nt) & (nvp < tm))
    def _():
        def body(r, carry):
            src = ybuf.at[other, r // SUBLANES, pl.ds(r % SUBLANES, 1), :]
            pltpu.make_async_copy(src, hbm_row(out_hbm, dstp_ref, r), ssem.at[other]).start()
            return carry
        lax.fori_loop(0, nvp, body, 0)

    @pl.when(i < nt)
    def _():
        gather_wait(slot)

        @pl.when(i >= 2)
        def _():
            scatter_wait(slot, nv_ref[jnp.maximum(i - 2, 0)])

        @pl.when((i == 0) | (te_ref[i] != te_ref[jnp.maximum(i - 1, 0)]))
        def _():
            wg_s[...] = wg_ref[...].astype(BF16)
            wu_s[...] = wu_ref[...].astype(BF16)
            wd_s[...] = wd_ref[...].astype(BF16)

        def step(scatter_prev):
            gather_issue(srcn_ref, other)
            if scatter_prev:
                scatter_issue_full(other)
            x = xbuf[slot].reshape(tm, D).astype(BF16)
            g = jnp.dot(x, wg_s[...], preferred_element_type=F32)
            u = jnp.dot(x, wu_s[...], preferred_element_type=F32)
            hid = (_silu(g) * u).astype(BF16)
            y = jnp.dot(hid, wd_s[...], preferred_element_type=F32) * gate_ref[...]
            ybuf[slot] = y.reshape(tm // SUBLANES, SUBLANES, D)

        full_prev = (i >= 1) & (nvp == tm)

        @pl.when(full_prev)
        def _():
            step(True)

        @pl.when(jnp.logical_not(full_prev))
        def _():
            step(False)

    @pl.when(i == nt)
    def _():
        @pl.when(nvp == tm)
        def _():
            scatter_issue_full(other)
        gather_wait(slot)

        @pl.when(nt >= 2)
        def _():
            scatter_wait(slot, nv_ref[jnp.maximum(nt - 2, 0)])
        scatter_wait(other, nvp)


def _experts(h2, plan, w_gate, w_up, w_down, layer, tm):
    te, nt, nv, src, dst, gate = plan
    N, D = h2.shape
    _, E, _, F = w_gate.shape
    T = src.shape[0]
    assert tm & (tm - 1) == 0 and tm >= SUBLANES and N % SUBLANES == 0 and N >= tm
    smem_blk = lambda m: pl.BlockSpec((1, 1, 2 * tm), m, memory_space=pltpu.SMEM)
    clamp = lambda v: jnp.clip(v, 0, T - 1)
    wsel = lambda i, te, nt, nv: (layer, te[clamp(i)], 0, 0)
    grid_spec = pltpu.PrefetchScalarGridSpec(
        num_scalar_prefetch=3,
        grid=(T + 1,),
        in_specs=[
            smem_blk(lambda i, te, nt, nv: (0, 0, 0)),
            smem_blk(lambda i, te, nt, nv: (clamp(i + 1), 0, 0)),
            smem_blk(lambda i, te, nt, nv: (clamp(i - 1), 0, 0)),
            pl.BlockSpec((tm, 1), lambda i, te, nt, nv: (clamp(i), 0)),
            pl.BlockSpec((None, None, D, F), wsel),
            pl.BlockSpec((None, None, D, F), wsel),
            pl.BlockSpec((None, None, F, D), wsel),
            pl.BlockSpec(memory_space=pl.ANY),
        ],
        out_specs=pl.BlockSpec(memory_space=pl.ANY),
        scratch_shapes=[
            pltpu.VMEM((2, tm // SUBLANES, SUBLANES, D), F32),
            pltpu.VMEM((2, tm // SUBLANES, SUBLANES, D), F32),
            pltpu.VMEM((D, F), BF16),
            pltpu.VMEM((D, F), BF16),
            pltpu.VMEM((F, D), BF16),
            pltpu.SemaphoreType.DMA((2,)),
            pltpu.SemaphoreType.DMA((2,)),
        ],
    )
    return pl.pallas_call(
        functools.partial(_moe_kernel, tm=tm),
        out_shape=jax.ShapeDtypeStruct((N // SUBLANES * TOP_K, SUBLANES, D), F32),
        grid_spec=grid_spec,
        compiler_params=_cparams(("arbitrary",)),
        name="moe_experts",
    )(te, nt, nv, src, src, dst, gate, w_gate, w_up, w_down,
      h2.reshape(N // SUBLANES, SUBLANES, D))


def _combine_kernel(h2_ref, x1_ref, po_ref, wsg_ref, wsu_ref, wsd_ref, ga2_ref, o_ref):
    h = h2_ref[...].astype(BF16)
    g = jnp.dot(h, wsg_ref[...], preferred_element_type=F32)
    u = jnp.dot(h, wsu_ref[...], preferred_element_type=F32)
    acc = jnp.dot((_silu(g) * u).astype(BF16), wsd_ref[...], preferred_element_type=F32)
    tm, D = acc.shape
    groups = []
    for grp in range(tm // SUBLANES):
        tot = po_ref[grp * TOP_K]
        for k in range(1, TOP_K):
            tot = tot + po_ref[grp * TOP_K + k]
        groups.append(tot)
    routed = jnp.concatenate(groups, axis=0)
    o_ref[...] = x1_ref[...] + ga2_ref[...] * (routed + acc)


def _combine(h2, x1, pair_out, wsg, wsu, wsd, modv, layer, S, tm):
    N, D = x1.shape
    F = wsg.shape[1]
    tpb = S // tm
    return pl.pallas_call(
        _combine_kernel,
        out_shape=jax.ShapeDtypeStruct((N, D), F32),
        grid=(N // tm,),
        in_specs=[
            pl.BlockSpec((tm, D), lambda i: (i, 0)),
            pl.BlockSpec((tm, D), lambda i: (i, 0)),
            pl.BlockSpec((tm // SUBLANES * TOP_K, SUBLANES, D), lambda i: (i, 0, 0)),
            pl.BlockSpec((D, F), lambda i: (0, 0)),
            pl.BlockSpec((D, F), lambda i: (0, 0)),
            pl.BlockSpec((F, D), lambda i: (0, 0)),
            _mod_spec(layer, 5, lambda i: i // tpb, D, 1),
        ],
        out_specs=pl.BlockSpec((tm, D), lambda i: (i, 0)),
        compiler_params=_cparams(("parallel",)),
        name="shared_combine",
    )(h2, x1, pair_out, wsg, wsu, wsd, modv)


def _moe_block(x1, h2, lgt, layer, S, modv, b_router, w_gate, w_up, w_down, ws_gate, ws_up, ws_down,
               tiles):
    N, D = x1.shape
    E = b_router.shape[1]
    tm = tiles["expert"]
    eidx, gates = _route(lgt, b_router[layer], _tile(N, tiles["route"]))
    plan = _dispatch_plan(eidx, gates, E, tm, (N * TOP_K) // tm + E)
    pair_out = _experts(h2, plan, w_gate, w_up, w_down, layer, tm)
    return _combine(h2, x1, pair_out, ws_gate[layer].astype(BF16), ws_up[layer].astype(BF16),
                    ws_down[layer].astype(BF16), modv, layer, S, _tile(S, tiles["combine"]))


DEFAULT_TILES = dict(qkv=512, attn=256, oproj=256, pool=256, route=512, expert=256, combine=128)


def kernel(x, c, ctx, c_ctx, w_ada, b_ada, g_norm1, g_norm2, w_qkv, g_q, g_k, sink, w_o, w_pool,
           pool_scale, w_router, b_router, w_gate, w_up, w_down, ws_gate, ws_up, ws_down):
    return _forward(DEFAULT_TILES, x, c, ctx, c_ctx, w_ada, b_ada, g_norm1, g_norm2, w_qkv, g_q, g_k,
                    sink, w_o, w_pool, pool_scale, w_router, b_router, w_gate, w_up, w_down,
                    ws_gate, ws_up, ws_down)


def _forward(tiles, x, c, ctx, c_ctx, w_ada, b_ada, g_norm1, g_norm2, w_qkv, g_q, g_k, sink, w_o,
             w_pool, pool_scale, w_router, b_router, w_gate, w_up, w_down, ws_gate, ws_up, ws_down):
    B, S, D = x.shape
    C = ctx.shape[1]
    L = w_ada.shape[0]
    assert L == 2 and B + 1 <= SUBLANES, "two-layer trunk: attention layer then pooling layer"
    N = B * S
    qd = w_o.shape[1]
    kd = (w_qkv.shape[2] - qd) // 2

    cond8 = jnp.zeros((SUBLANES, D), F32).at[:B].set(c).at[B].set(c_ctx)
    mod = _adaln(cond8, w_ada, b_ada)
    modv = mod.reshape(L, SUBLANES, 6, D).transpose(0, 2, 1, 3).reshape(L, 6, SUBLANES, 1, D)
    wr = _router_weights(w_router)
    x2 = x.reshape(N, D)

    scale = HEAD_DIM ** -0.5
    wq = w_qkv[0].astype(BF16)
    ones_v = jnp.ones((kd,), F32)
    gain_lat = jnp.concatenate([jnp.tile(g_q[0] * scale, qd // HEAD_DIM),
                                jnp.tile(g_k[0], kd // HEAD_DIM), ones_v]).reshape(1, -1)
    gain_ctx = jnp.concatenate([jnp.tile(g_k[0], kd // HEAD_DIM), ones_v]).reshape(1, -1)
    tmq = _tile(S, tiles["qkv"])
    qkv = _qkv_proj(x2, g_norm1[0], modv, 0, lambda i: i // (S // tmq), wq, gain_lat, qd + kd,
                    _rope_tables(S), tmq)
    kvc = _qkv_proj(ctx.reshape(B * C, D), g_norm1[0], modv, 0, lambda i: B, wq[:, qd:], gain_ctx,
                    kd, None, C)
    o = _attention(qkv, kvc, sink[0], B, S, _tile(S, tiles["attn"]))
    x1, h2, lgt = _oproj(o, x2, w_o[0].astype(BF16), g_norm2[0], modv, 0, S, wr,
                         _tile(S, tiles["oproj"]))
    moe_w = (modv, b_router, w_gate, w_up, w_down, ws_gate, ws_up, ws_down, tiles)
    x2 = _moe_block(x1, h2, lgt, 0, S, *moe_w)

    x1, h2, lgt = _pool_mixer(x2, g_norm1[1], g_norm2[1], modv, 1, B, S, w_pool[0].astype(BF16),
                              pool_scale[0], wr, _tile(S, tiles["pool"]))
    x2 = _moe_block(x1, h2, lgt, 1, S, *moe_w)
    return x2.reshape(B, S, D)
```

```python
import functools

import jax
import jax.numpy as jnp
from jax import lax
from jax.experimental import pallas as pl
from jax.experimental.pallas import tpu as pltpu

HEAD_DIM = 128
GQA_GROUP = 4
GRID_W = 64
WINDOW = 128
ROPE_BASE = 10000.0
POOL_WINDOWS = (2, 4, 8, 16)
POOL_HALO = 8
N_EXPERT_GROUPS = 8
TOPK_GROUPS = 4
TOP_K = 8
ROUTED_SCALE = 2.5
EPS = 1e-6
NEG = -1e30

LANES = 128
SUBLANES = 8
VMEM_LIMIT = 52 * 1024 * 1024
VMEM_LIMIT_EXPERTS = 58 * 1024 * 1024

F32 = jnp.float32
BF16 = jnp.bfloat16


def _cparams(sem):
    return pltpu.CompilerParams(dimension_semantics=sem, vmem_limit_bytes=VMEM_LIMIT)


def _tile(n, want):
    t = min(n, want)
    while n % t:
        t //= 2
    return t


def _silu(v):
    return v * jax.nn.sigmoid(v)


def _rms_mod(x, g, shift, scale):
    ms = jnp.mean(x * x, axis=-1, keepdims=True)
    return (x * lax.rsqrt(ms + EPS) * g) * (1.0 + scale) + shift


def _adaln_kernel(c_ref, w_ref, b_ref, o_ref):
    a = _silu(c_ref[...])
    o_ref[...] = jnp.dot(a.astype(BF16), w_ref[...].astype(BF16),
                         preferred_element_type=F32) + b_ref[...]


def _adaln(cond8, w_ada, b_ada):
    L, D, D6 = w_ada.shape
    tn = _tile(D6, 1024)
    return pl.pallas_call(
        _adaln_kernel,
        out_shape=jax.ShapeDtypeStruct((L, SUBLANES, D6), F32),
        grid=(L, D6 // tn),
        in_specs=[
            pl.BlockSpec((SUBLANES, D), lambda l, j: (0, 0)),
            pl.BlockSpec((None, D, tn), lambda l, j: (l, 0, j)),
            pl.BlockSpec((None, 1, tn), lambda l, j: (l, 0, j)),
        ],
        out_specs=pl.BlockSpec((None, SUBLANES, tn), lambda l, j: (l, 0, j)),
        compiler_params=_cparams(("parallel", "parallel")),
        name="adaln",
    )(cond8, w_ada, b_ada.reshape(L, 1, D6))


def _rope(a, cos, sin_signed):
    lane = lax.broadcasted_iota(jnp.int32, a.shape, 1)
    partner = jnp.where((lane & 32) == 0, pltpu.roll(a, 96, 1), pltpu.roll(a, 32, 1))
    return a * cos + partner * sin_signed


def _qkv_kernel(*refs, n_norm_tiles, rope):
    if rope:
        x_ref, g1_ref, sh_ref, sc_ref, w_ref, gain_ref, cos_ref, sin_ref, o_ref, h_scr = refs
    else:
        x_ref, g1_ref, sh_ref, sc_ref, w_ref, gain_ref, o_ref, h_scr = refs
    j = pl.program_id(1)

    @pl.when(j == 0)
    def _():
        h_scr[...] = _rms_mod(x_ref[...], g1_ref[...], sh_ref[...], sc_ref[...]).astype(BF16)

    acc = jnp.dot(h_scr[...], w_ref[...], preferred_element_type=F32)

    @pl.when(j < n_norm_tiles)
    def _():
        for hh in range(acc.shape[1] // HEAD_DIM):
            sl = slice(hh * HEAD_DIM, (hh + 1) * HEAD_DIM)
            a = acc[:, sl]
            ms = jnp.mean(a * a, axis=-1, keepdims=True)
            a = a * lax.rsqrt(ms + EPS) * gain_ref[:, sl]
            if rope:
                a = _rope(a, cos_ref[...], sin_ref[...])
            o_ref[:, sl] = a.astype(o_ref.dtype)

    @pl.when(j >= n_norm_tiles)
    def _():
        o_ref[...] = acc.astype(o_ref.dtype)


def _qkv_proj(x2, g1, modv, layer, mod_row_fn, w_bf16, gain, n_norm_cols, rope_tabs, tm):
    N, D = x2.shape
    ncols = w_bf16.shape[1]
    tn = _tile(n_norm_cols, 512)
    assert ncols % tn == 0 and tn % HEAD_DIM == 0
    rope = rope_tabs is not None
    in_specs = [
        pl.BlockSpec((tm, D), lambda i, j: (i, 0)),
        pl.BlockSpec((1, D), lambda i, j: (0, 0)),
        pl.BlockSpec((None, None, None, 1, D), lambda i, j: (layer, 0, mod_row_fn(i), 0, 0)),
        pl.BlockSpec((None, None, None, 1, D), lambda i, j: (layer, 1, mod_row_fn(i), 0, 0)),
        pl.BlockSpec((D, tn), lambda i, j: (0, j)),
        pl.BlockSpec((1, tn), lambda i, j: (0, j)),
    ]
    args = [x2, g1.reshape(1, D), modv, modv, w_bf16, gain]
    if rope:
        cos, sin = rope_tabs
        tpb = cos.shape[0] // tm
        in_specs += [pl.BlockSpec((tm, HEAD_DIM), lambda i, j: (i % tpb, 0))] * 2
        args += [cos, sin]
    return pl.pallas_call(
        functools.partial(_qkv_kernel, n_norm_tiles=n_norm_cols // tn, rope=rope),
        out_shape=jax.ShapeDtypeStruct((N, ncols), BF16),
        grid=(N // tm, ncols // tn),
        in_specs=in_specs,
        out_specs=pl.BlockSpec((tm, tn), lambda i, j: (i, j)),
        scratch_shapes=[pltpu.VMEM((tm, D), BF16)],
        compiler_params=_cparams(("parallel", "arbitrary")),
        name="qkv_rope" if rope else "ctx_kv",
    )(*args)


def _rope_tables(S):
    rows = S // GRID_W
    row = jnp.repeat(jnp.arange(rows), GRID_W).astype(F32)
    col = jnp.tile(jnp.arange(GRID_W), rows).astype(F32)
    nf = HEAD_DIM // 4
    inv = ROPE_BASE ** (-jnp.arange(nf, dtype=F32) / nf)
    ar, ac = row[:, None] * inv, col[:, None] * inv
    cos = jnp.concatenate([jnp.cos(ar), jnp.cos(ar), jnp.cos(ac), jnp.cos(ac)], axis=1)
    sin = jnp.concatenate([-jnp.sin(ar), jnp.sin(ar), -jnp.sin(ac), jnp.sin(ac)], axis=1)
    return cos, sin


def _attn_kernel(sink_ref, bias_ref, q_ref, kp_ref, kc_ref, kn_ref, vp_ref, vc_ref, vn_ref, kx_ref,
                 vx_ref, o_ref, k_scr, v_scr, *, tq):
    kh = pl.program_id(2)
    nloc = tq + 2 * WINDOW
    nctx = kx_ref.shape[0]
    for scr, prev, cur, nxt, cx in ((k_scr, kp_ref, kc_ref, kn_ref, kx_ref),
                                    (v_scr, vp_ref, vc_ref, vn_ref, vx_ref)):
        scr[0:WINDOW, :] = prev[...]
        scr[WINDOW:WINDOW + tq, :] = cur[...]
        scr[WINDOW + tq:nloc, :] = nxt[...]
        scr[nloc:nloc + nctx, :] = cx[...]
    kall = k_scr[...]
    vall = v_scr[...]
    for g in range(GQA_GROUP):
        sl = slice(g * HEAD_DIM, (g + 1) * HEAD_DIM)
        s = lax.dot_general(q_ref[:, sl], kall, (((1,), (1,)), ((), ())),
                            preferred_element_type=F32)
        s = s + bias_ref[...]
        sink = sink_ref[kh * GQA_GROUP + g]
        m = jnp.maximum(jnp.max(s, axis=-1, keepdims=True), sink)
        p = jnp.exp(s - m)
        denom = jnp.sum(p, axis=-1, keepdims=True) + jnp.exp(sink - m)
        o = jnp.dot(p.astype(BF16), vall, preferred_element_type=F32)
        o_ref[:, sl] = (o / denom).astype(o_ref.dtype)


def _attention(qkv, kvc, sink, B, S, tq):
    N, ncols = qkv.shape
    C = kvc.shape[0] // B
    KV = kvc.shape[1] // (2 * HEAD_DIM)
    H = KV * GQA_GROUP
    assert ncols == (H + 2 * KV) * HEAD_DIM and tq % WINDOW == 0 and S % tq == 0
    tpb, wpb, wpt = S // tq, S // WINDOW, tq // WINDOW
    gw = GQA_GROUP * HEAD_DIM

    def prev_map(col0):
        return lambda b, i, k: (jnp.maximum(b * wpb + i * wpt - 1, b * wpb), col0 + k)

    def cur_map(col0):
        return lambda b, i, k: (b * tpb + i, col0 + k)

    def next_map(col0):
        return lambda b, i, k: (jnp.minimum(b * wpb + (i + 1) * wpt, (b + 1) * wpb - 1), col0 + k)

    halo = lambda m: pl.BlockSpec((WINDOW, HEAD_DIM), m)
    cur = lambda m: pl.BlockSpec((tq, HEAD_DIM), m)
    nkeys = tq + 2 * WINDOW + C
    r = jnp.arange(tq)[:, None]
    c = jnp.arange(nkeys)[None, :]
    band = jnp.abs(r - (c - WINDOW)) <= WINDOW
    variants = [band & ((c >= WINDOW) | (not first)) & ((c < tq + WINDOW) | (not last))
                for last in (False, True) for first in (False, True)]
    bias = jnp.where(jnp.stack(variants) | (c >= tq + 2 * WINDOW), 0.0, NEG).astype(F32)
    in_specs = [
        pl.BlockSpec(memory_space=pltpu.SMEM),
        pl.BlockSpec((None, tq, nkeys),
                     lambda b, i, k: ((i == 0).astype(jnp.int32)
                                      + 2 * (i == tpb - 1).astype(jnp.int32), 0, 0)),
        pl.BlockSpec((tq, gw), lambda b, i, k: (b * tpb + i, k)),
        halo(prev_map(H)), cur(cur_map(H)), halo(next_map(H)),
        halo(prev_map(H + KV)), cur(cur_map(H + KV)), halo(next_map(H + KV)),
        pl.BlockSpec((C, HEAD_DIM), lambda b, i, k: (b, k)),
        pl.BlockSpec((C, HEAD_DIM), lambda b, i, k: (b, KV + k)),
    ]
    return pl.pallas_call(
        functools.partial(_attn_kernel, tq=tq),
        out_shape=jax.ShapeDtypeStruct((N, H * HEAD_DIM), BF16),
        grid=(B, tpb, KV),
        in_specs=in_specs,
        out_specs=pl.BlockSpec((tq, gw), lambda b, i, k: (b * tpb + i, k)),
        scratch_shapes=[pltpu.VMEM((nkeys, HEAD_DIM), BF16), pltpu.VMEM((nkeys, HEAD_DIM), BF16)],
        compiler_params=_cparams(("parallel", "parallel", "parallel")),
        name="band_attn",
    )(sink, bias, qkv, qkv, qkv, qkv, qkv, qkv, qkv, kvc, kvc)


def _split_bf16(v):
    hi = v.astype(BF16)
    return hi, (v - hi.astype(F32)).astype(BF16)


def _post_mixer(x1, g2_ref, sh2_ref, sc2_ref, wr_ref, x1_ref, h2_ref, h2r_ref, lg_ref):
    x1_ref[...] = x1
    h2 = _rms_mod(x1, g2_ref[...], sh2_ref[...], sc2_ref[...])
    h2_ref[...] = h2.astype(h2_ref.dtype)
    h2r_ref[...] = h2.reshape(h2r_ref.shape)
    hi, lo = _split_bf16(h2)
    dot = functools.partial(jnp.dot, preferred_element_type=F32)
    lg_ref[...] = dot(hi, wr_ref[0]) + (dot(hi, wr_ref[1]) + dot(lo, wr_ref[0]))


def _oproj_kernel(o_ref, x_ref, wo_ref, ga1_ref, g2_ref, sh2_ref, sc2_ref, wr_ref, *out_refs):
    y = jnp.dot(o_ref[...], wo_ref[...], preferred_element_type=F32)
    _post_mixer(x_ref[...] + ga1_ref[...] * y, g2_ref, sh2_ref, sc2_ref, wr_ref, *out_refs)


def _mod_spec(layer, chunk, row_fn, D, nargs):
    if nargs == 1:
        return pl.BlockSpec((None, None, None, 1, D), lambda i: (layer, chunk, row_fn(i), 0, 0))
    return pl.BlockSpec((None, None, None, 1, D), lambda b, i: (layer, chunk, b, 0, 0))


def _post_mixer_out(N, D, tm, row_map):
    shapes = (jax.ShapeDtypeStruct((N, D), F32), jax.ShapeDtypeStruct((N, D), BF16),
              jax.ShapeDtypeStruct((N, 1, D), F32), jax.ShapeDtypeStruct((N, LANES), F32))
    specs = (pl.BlockSpec((tm, D), lambda *a: (row_map(*a), 0)),
             pl.BlockSpec((tm, D), lambda *a: (row_map(*a), 0)),
             pl.BlockSpec((tm, 1, D), lambda *a: (row_map(*a), 0, 0)),
             pl.BlockSpec((tm, LANES), lambda *a: (row_map(*a), 0)))
    return shapes, specs


def _router_weights(w_router):
    L, D, E = w_router.shape
    assert E <= LANES
    w = jnp.pad(w_router, ((0, 0), (0, 0), (0, LANES - E)))
    hi = w.astype(BF16)
    lo = (w - hi.astype(F32)).astype(BF16)
    return jnp.stack([hi, lo], axis=1)


def _oproj(o, x2, wo_bf16, g2, modv, layer, S, wr, tm):
    N, D = x2.shape
    qd = o.shape[1]
    tpb = S // tm
    row = lambda i: i // tpb
    shapes, specs = _post_mixer_out(N, D, tm, lambda i: i)
    return pl.pallas_call(
        _oproj_kernel,
        out_shape=shapes,
        grid=(N // tm,),
        in_specs=[
            pl.BlockSpec((tm, qd), lambda i: (i, 0)),
            pl.BlockSpec((tm, D), lambda i: (i, 0)),
            pl.BlockSpec((qd, D), lambda i: (0, 0)),
            _mod_spec(layer, 2, row, D, 1),
            pl.BlockSpec((1, D), lambda i: (0, 0)),
            _mod_spec(layer, 3, row, D, 1),
            _mod_spec(layer, 4, row, D, 1),
            pl.BlockSpec((None, 2, D, LANES), lambda i: (layer, 0, 0, 0)),
        ],
        out_specs=specs,
        compiler_params=_cparams(("parallel",)),
        name="oproj_norm_router",
    )(o, x2, wo_bf16, modv, g2.reshape(1, D), modv, modv, wr)


def _pool_kernel(xp_ref, xc_ref, xn_ref, g1_ref, sh1_ref, sc1_ref, wp_ref, ps_ref, ga1_ref,
                 g2_ref, sh2_ref, sc2_ref, wr_ref, x1_ref, h2_ref, h2r_ref, lg_ref, halo_scr,
                 *, ts, seq):
    i = pl.program_id(1)
    pos0 = i * ts
    norm = lambda v: _rms_mod(v, g1_ref[...], sh1_ref[...], sc1_ref[...])
    xc = xc_ref[...]
    hc = norm(xc)
    hp = jnp.where(pos0 > 0, norm(xp_ref[...]), 0.0)
    hn = jnp.where(pos0 + ts < seq, norm(xn_ref[...]), 0.0)
    halo_scr[...] = jnp.zeros_like(halo_scr)
    halo_scr[0:POOL_HALO, :] = hp
    halo_scr[POOL_HALO:2 * POOL_HALO, :] = hn
    hc16 = hc.astype(BF16)
    halo16 = halo_scr[...].astype(BF16)

    r_c = lax.broadcasted_iota(jnp.int32, (ts, ts), 0)
    c_c = lax.broadcasted_iota(jnp.int32, (ts, ts), 1)
    r_h = lax.broadcasted_iota(jnp.int32, (ts, LANES), 0)
    c_h = lax.broadcasted_iota(jnp.int32, (ts, LANES), 1)
    p_h = jnp.where(c_h < POOL_HALO, c_h - POOL_HALO, ts + c_h - POOL_HALO)
    in_halo = c_h < 2 * POOL_HALO
    pos = pos0 + lax.broadcasted_iota(jnp.int32, (ts, 1), 0)
    gw = xc.shape[1] // len(POOL_WINDOWS)
    ys = []
    for g, w in enumerate(POOL_WINDOWS):
        lo, hi = w // 2, w - w // 2
        sl = slice(g * gw, (g + 1) * gw)
        band_c = ((c_c >= r_c - lo) & (c_c < r_c + hi)).astype(BF16)
        band_h = (in_halo & (p_h >= r_h - lo) & (p_h < r_h + hi)).astype(BF16)
        tot = (jnp.dot(band_c, hc16[:, sl], preferred_element_type=F32)
               + jnp.dot(band_h, halo16[:, sl], preferred_element_type=F32))
        cnt = (jnp.minimum(pos + hi, seq) - jnp.maximum(pos - lo, 0)).astype(F32)
        d = tot / cnt - hc[:, sl]
        ys.append(jnp.dot(d.astype(BF16), wp_ref[g], preferred_element_type=F32))
    y = jnp.concatenate(ys, axis=1) * ps_ref[...]
    _post_mixer(xc + ga1_ref[...] * y, g2_ref, sh2_ref, sc2_ref, wr_ref,
                x1_ref, h2_ref, h2r_ref, lg_ref)


def _pool_mixer(x2, g1, g2, modv, layer, B, S, wp_bf16, pool_scale, wr, ts):
    N, D = x2.shape
    G, gw, _ = wp_bf16.shape
    tpb = S // ts
    hpt, hpb = ts // POOL_HALO, S // POOL_HALO
    shapes, specs = _post_mixer_out(N, D, ts, lambda b, i: b * tpb + i)
    mod = lambda chunk: _mod_spec(layer, chunk, None, D, 2)
    vec = lambda: pl.BlockSpec((1, D), lambda b, i: (0, 0))
    return pl.pallas_call(
        functools.partial(_pool_kernel, ts=ts, seq=S),
        out_shape=shapes,
        grid=(B, tpb),
        in_specs=[
            pl.BlockSpec((POOL_HALO, D), lambda b, i: (jnp.maximum(b * hpb + i * hpt - 1, b * hpb), 0)),
            pl.BlockSpec((ts, D), lambda b, i: (b * tpb + i, 0)),
            pl.BlockSpec((POOL_HALO, D),
                         lambda b, i: (jnp.minimum(b * hpb + (i + 1) * hpt, (b + 1) * hpb - 1), 0)),
            vec(), mod(0), mod(1),
            pl.BlockSpec((G, gw, gw), lambda b, i: (0, 0, 0)),
            vec(), mod(2), vec(), mod(3), mod(4),
            pl.BlockSpec((None, 2, D, LANES), lambda b, i: (layer, 0, 0, 0)),
        ],
        out_specs=specs,
        scratch_shapes=[pltpu.VMEM((LANES, D), F32)],
        compiler_params=_cparams(("parallel", "parallel")),
        name="pool_norm_router",
    )(x2, x2, x2, g1.reshape(1, D), modv, modv, wp_bf16, pool_scale.reshape(1, D), modv,
      g2.reshape(1, D), modv, modv, wr)


def _first_argmax(v, idx, big):
    m = jnp.max(v, axis=0, keepdims=True)
    first = jnp.min(jnp.where(v == m, idx, big), axis=0, keepdims=True)
    return m, first


def _route_kernel(lg_ref, b_ref, eidx_ref, gate_ref):
    E = b_ref.shape[0]
    s = jax.nn.sigmoid(lg_ref[...].T[0:E, :])
    sb = s + b_ref[...]
    tn = s.shape[1]
    per = E // N_EXPERT_GROUPS
    sub = lax.broadcasted_iota(jnp.int32, (per, tn), 0).astype(F32)
    gid = lax.broadcasted_iota(jnp.int32, (N_EXPERT_GROUPS, tn), 0).astype(F32)
    gs = jnp.zeros((N_EXPERT_GROUPS, tn), F32)
    for g in range(N_EXPERT_GROUPS):
        blk = sb[g * per:(g + 1) * per, :]
        m1, i1 = _first_argmax(blk, sub, float(per))
        m2 = jnp.max(jnp.where(sub == i1, -jnp.inf, blk), axis=0, keepdims=True)
        gs = jnp.where(gid == float(g), m1 + m2, gs)
    gsel = jnp.zeros_like(gs)
    for _ in range(TOPK_GROUPS):
        _, gi = _first_argmax(gs, gid, float(N_EXPERT_GROUPS))
        hit = gid == gi
        gsel = jnp.where(hit, 1.0, gsel)
        gs = jnp.where(hit, -jnp.inf, gs)
    cur = jnp.concatenate(
        [jnp.where(gsel[g:g + 1, :] > 0.0, sb[g * per:(g + 1) * per, :], NEG)
         for g in range(N_EXPERT_GROUPS)], axis=0)
    eid = lax.broadcasted_iota(jnp.int32, (E, tn), 0).astype(F32)
    sels = []
    for k in range(TOP_K):
        _, ei = _first_argmax(cur, eid, float(E))
        hit = eid == ei
        eidx_ref[k:k + 1, :] = ei.astype(jnp.int32)
        sels.append(jnp.sum(jnp.where(hit, s, 0.0), axis=0, keepdims=True))
        cur = jnp.where(hit, -jnp.inf, cur)
    denom = sels[0]
    for k in range(1, TOP_K):
        denom = denom + sels[k]
    for k in range(TOP_K):
        gate_ref[k:k + 1, :] = sels[k] / denom * ROUTED_SCALE


def _route(lg, b_router, tn):
    N = lg.shape[0]
    E = b_router.shape[0]
    return pl.pallas_call(
        _route_kernel,
        out_shape=(jax.ShapeDtypeStruct((TOP_K, N), jnp.int32),
                   jax.ShapeDtypeStruct((TOP_K, N), F32)),
        grid=(N // tn,),
        in_specs=[pl.BlockSpec((tn, LANES), lambda i: (i, 0)),
                  pl.BlockSpec((E, 1), lambda i: (0, 0))],
        out_specs=(pl.BlockSpec((TOP_K, tn), lambda i: (0, i)),
                   pl.BlockSpec((TOP_K, tn), lambda i: (0, i))),
        compiler_params=_cparams(("parallel",)),
        name="route_topk",
    )(lg, b_router.reshape(E, 1))


def _dispatch_plan(eidx, gates, E, tm, n_tiles_max):
    K, N = eidx.shape
    P = N * K
    flat_e = eidx.T.reshape(P)
    flat_g = gates.T.reshape(P)
    perm = jnp.argsort(flat_e, stable=True).astype(jnp.int32)
    counts = jnp.sum((flat_e[:, None] == jnp.arange(E, dtype=jnp.int32)[None, :]).astype(jnp.int32),
                     axis=0)
    tiles_e = (counts + tm - 1) // tm
    tile_end = jnp.cumsum(tiles_e)
    tile_start = tile_end - tiles_e
    n_tiles = tile_end[-1]
    cs = jnp.cumsum(counts) - counts
    tj = jnp.arange(n_tiles_max, dtype=jnp.int32)
    te = jnp.minimum(jnp.sum((tj[:, None] >= tile_end[None, :]).astype(jnp.int32), axis=1), E - 1)
    te = jnp.where(tj < n_tiles, te, te[jnp.maximum(n_tiles - 1, 0)])
    first = (tj - tile_start[te]) * tm
    nvalid = jnp.where(tj < n_tiles, jnp.clip(counts[te] - first, 0, tm), 0)
    r = jnp.arange(tm, dtype=jnp.int32)[None, :]
    valid = r < nvalid[:, None]
    pair = jnp.where(valid, perm[jnp.clip((cs[te] + first)[:, None] + r, 0, P - 1)], 0)
    gate = jnp.where(valid, flat_g[pair], 0.0)
    tok, k = pair // K, pair % K
    i32 = lambda v: v.astype(jnp.int32)
    shape3 = (n_tiles_max, 1, tm)
    return (i32(te), i32(n_tiles).reshape(1), i32(nvalid), i32(tok).reshape(shape3),
            i32(k * N + tok).reshape(shape3), gate.reshape(n_tiles_max * tm, 1))


ROW_BUFS = 3


def _moe_kernel(te_ref, nt_ref, nv_ref, src0_ref, src1_ref, srcn_ref, dstp_ref, gate_ref,
                wg_ref, wu_ref, wd_ref, h_hbm, out_hbm,
                xbuf, ybuf, x2d, wg_s, wu_s, wd_s, gsem, ssem, *, tm):
    i = pl.program_id(0)
    nt = nt_ref[0]
    slot = lax.rem(i, ROW_BUFS)
    other = lax.rem(i + 2, ROW_BUFS)
    D = xbuf.shape[3]
    tile_nv = lambda j: nv_ref[jnp.maximum(j, 0)]
    nvp = tile_nv(i - 1)

    def gather_issue(idx_smem, s):
        for r in range(tm):
            pltpu.make_async_copy(h_hbm.at[idx_smem[0, 0, r]], xbuf.at[s, r],
                                  gsem.at[s]).start(priority=r % 2)

    def gather_wait(s):
        pltpu.make_async_copy(h_hbm.at[pl.ds(0, tm)], xbuf.at[s], gsem.at[s]).wait()

    def scatter_issue_full(s):
        for r in range(tm):
            pltpu.make_async_copy(ybuf.at[s, r], out_hbm.at[dstp_ref[0, 0, r]],
                                  ssem.at[s]).start(priority=r % 2)

    def scatter_wait(s, nv):
        p = tm
        while p >= 1:
            @pl.when((nv & p) != 0)
            def _(p=p):
                pltpu.make_async_copy(ybuf.at[s, pl.ds(0, p)], out_hbm.at[pl.ds(0, p)],
                                      ssem.at[s]).wait()
            p //= 2

    @pl.when(i == 0)
    def _():
        gather_issue(src0_ref, 0)
        gather_issue(src1_ref, 1)

    @pl.when((i >= 1) & (i <= nt) & (nvp < tm))
    def _():
        def body(r, carry):
            pltpu.make_async_copy(ybuf.at[other, r], out_hbm.at[dstp_ref[0, 0, r]],
                                  ssem.at[other]).start()
            return carry
        lax.fori_loop(0, nvp, body, 0)

    @pl.when(i < nt)
    def _():
        gather_wait(slot)

        @pl.when(i >= ROW_BUFS)
        def _():
            scatter_wait(slot, tile_nv(i - ROW_BUFS))

        @pl.when((i == 0) | (te_ref[i] != te_ref[jnp.maximum(i - 1, 0)]))
        def _():
            wg_s[...] = wg_ref[...].astype(BF16)
            wu_s[...] = wu_ref[...].astype(BF16)
            wd_s[...] = wd_ref[...].astype(BF16)

        def step(scatter_prev):
            gather_issue(srcn_ref, other)
            if scatter_prev:
                scatter_issue_full(other)
            x2d[...] = xbuf[slot].reshape(tm, D)
            x = x2d[...].astype(BF16)
            g = jnp.dot(x, wg_s[...], preferred_element_type=F32)
            u = jnp.dot(x, wu_s[...], preferred_element_type=F32)
            hid = (_silu(g) * u).astype(BF16)
            y = jnp.dot(hid, wd_s[...], preferred_element_type=F32) * gate_ref[...]
            ybuf[slot] = y.reshape(tm, 1, D)

        full_prev = (i >= 1) & (nvp == tm)

        @pl.when(full_prev)
        def _():
            step(True)

        @pl.when(jnp.logical_not(full_prev))
        def _():
            step(False)

    @pl.when(i == nt)
    def _():
        @pl.when(nvp == tm)
        def _():
            scatter_issue_full(other)
        gather_wait(slot)
        gather_wait(lax.rem(i + 1, ROW_BUFS))
        for back in range(ROW_BUFS, 0, -1):
            @pl.when(nt >= back)
            def _(back=back):
                scatter_wait(lax.rem(nt - back + ROW_BUFS, ROW_BUFS), tile_nv(nt - back))


def _experts(h2r, plan, w_gate, w_up, w_down, layer, tm):
    te, nt, nv, src, dst, gate = plan
    N, _, D = h2r.shape
    _, E, _, F = w_gate.shape
    T = src.shape[0]
    assert tm & (tm - 1) == 0 and N >= tm and N * TOP_K >= tm
    smem_blk = lambda m: pl.BlockSpec((1, 1, tm), m, memory_space=pltpu.SMEM)
    clamp = lambda v: jnp.clip(v, 0, T - 1)
    wsel = lambda i, te, nt, nv: (layer, te[clamp(i)], 0, 0)
    grid_spec = pltpu.PrefetchScalarGridSpec(
        num_scalar_prefetch=3,
        grid=(T + 1,),
        in_specs=[
            smem_blk(lambda i, te, nt, nv: (0, 0, 0)),
            smem_blk(lambda i, te, nt, nv: (clamp(1), 0, 0)),
            smem_blk(lambda i, te, nt, nv: (clamp(i + 2), 0, 0)),
            smem_blk(lambda i, te, nt, nv: (clamp(i - 1), 0, 0)),
            pl.BlockSpec((tm, 1), lambda i, te, nt, nv: (clamp(i), 0)),
            pl.BlockSpec((None, None, D, F), wsel),
            pl.BlockSpec((None, None, D, F), wsel),
            pl.BlockSpec((None, None, F, D), wsel),
            pl.BlockSpec(memory_space=pl.ANY),
        ],
        out_specs=pl.BlockSpec(memory_space=pl.ANY),
        scratch_shapes=[
            pltpu.VMEM((ROW_BUFS, tm, 1, D), F32),
            pltpu.VMEM((ROW_BUFS, tm, 1, D), F32),
            pltpu.VMEM((tm, D), F32),
            pltpu.VMEM((D, F), BF16),
            pltpu.VMEM((D, F), BF16),
            pltpu.VMEM((F, D), BF16),
            pltpu.SemaphoreType.DMA((ROW_BUFS,)),
            pltpu.SemaphoreType.DMA((ROW_BUFS,)),
        ],
    )
    return pl.pallas_call(
        functools.partial(_moe_kernel, tm=tm),
        out_shape=jax.ShapeDtypeStruct((TOP_K * N, 1, D), F32),
        grid_spec=grid_spec,
        compiler_params=pltpu.CompilerParams(dimension_semantics=("arbitrary",),
                                             vmem_limit_bytes=VMEM_LIMIT_EXPERTS),
        name="moe_experts",
    )(te, nt, nv, src, src, src, dst, gate, w_gate, w_up, w_down, h2r)


def _combine_kernel(h2_ref, x1_ref, *refs):
    po_refs = refs[:TOP_K]
    wsg_ref, wsu_ref, wsd_ref, ga2_ref, o_ref, row_scr = refs[TOP_K:]
    h = h2_ref[...]
    g = jnp.dot(h, wsg_ref[...], preferred_element_type=F32)
    u = jnp.dot(h, wsu_ref[...], preferred_element_type=F32)
    acc = jnp.dot((_silu(g) * u).astype(BF16), wsd_ref[...], preferred_element_type=F32)
    for po_ref in po_refs:
        row_scr[...] = po_ref[...].reshape(row_scr.shape)
        acc = acc + row_scr[...]
    o_ref[...] = x1_ref[...] + ga2_ref[...] * acc


def _combine(h2, x1, pair_out, wsg, wsu, wsd, modv, layer, S, tm):
    N, D = x1.shape
    F = wsg.shape[1]
    tpb = S // tm
    nblk = N // tm
    slot_spec = lambda k: pl.BlockSpec((tm, 1, D), lambda i: (k * nblk + i, 0, 0))
    return pl.pallas_call(
        _combine_kernel,
        out_shape=jax.ShapeDtypeStruct((N, D), F32),
        grid=(nblk,),
        in_specs=[
            pl.BlockSpec((tm, D), lambda i: (i, 0)),
            pl.BlockSpec((tm, D), lambda i: (i, 0)),
            *[slot_spec(k) for k in range(TOP_K)],
            pl.BlockSpec((D, F), lambda i: (0, 0)),
            pl.BlockSpec((D, F), lambda i: (0, 0)),
            pl.BlockSpec((F, D), lambda i: (0, 0)),
            _mod_spec(layer, 5, lambda i: i // tpb, D, 1),
        ],
        out_specs=pl.BlockSpec((tm, D), lambda i: (i, 0)),
        scratch_shapes=[pltpu.VMEM((tm, D), F32)],
        compiler_params=_cparams(("parallel",)),
        name="shared_combine",
    )(h2, x1, *([pair_out] * TOP_K), wsg, wsu, wsd, modv)


def _moe_block(mixed, layer, S, modv, b_router, w_gate, w_up, w_down, ws_gate, ws_up, ws_down, tiles):
    x1, h2, h2r, lgt = mixed
    N, D = x1.shape
    E = b_router.shape[1]
    tm = tiles["expert"]
    eidx, gates = _route(lgt, b_router[layer], _tile(N, tiles["route"]))
    plan = _dispatch_plan(eidx, gates, E, tm, (N * TOP_K) // tm + E)
    pair_out = _experts(h2r, plan, w_gate, w_up, w_down, layer, tm)
    return _combine(h2, x1, pair_out, ws_gate[layer].astype(BF16), ws_up[layer].astype(BF16),
                    ws_down[layer].astype(BF16), modv, layer, S, _tile(S, tiles["combine"]))


DEFAULT_TILES = dict(qkv=512, attn=256, oproj=256, pool=256, route=512, expert=256, combine=128)


def kernel(x, c, ctx, c_ctx, w_ada, b_ada, g_norm1, g_norm2, w_qkv, g_q, g_k, sink, w_o, w_pool,
           pool_scale, w_router, b_router, w_gate, w_up, w_down, ws_gate, ws_up, ws_down):
    return _forward(DEFAULT_TILES, x, c, ctx, c_ctx, w_ada, b_ada, g_norm1, g_norm2, w_qkv, g_q, g_k,
                    sink, w_o, w_pool, pool_scale, w_router, b_router, w_gate, w_up, w_down,
                    ws_gate, ws_up, ws_down)


def _forward(tiles, x, c, ctx, c_ctx, w_ada, b_ada, g_norm1, g_norm2, w_qkv, g_q, g_k, sink, w_o,
             w_pool, pool_scale, w_router, b_router, w_gate, w_up, w_down, ws_gate, ws_up, ws_down):
    B, S, D = x.shape
    C = ctx.shape[1]
    L = w_ada.shape[0]
    assert L == 2 and B + 1 <= SUBLANES, "two-layer trunk: attention layer then pooling layer"
    N = B * S
    qd = w_o.shape[1]
    kd = (w_qkv.shape[2] - qd) // 2

    cond8 = jnp.zeros((SUBLANES, D), F32).at[:B].set(c).at[B].set(c_ctx)
    mod = _adaln(cond8, w_ada, b_ada)
    modv = mod.reshape(L, SUBLANES, 6, D).transpose(0, 2, 1, 3).reshape(L, 6, SUBLANES, 1, D)
    wr = _router_weights(w_router)
    x2 = x.reshape(N, D)

    scale = HEAD_DIM ** -0.5
    wq = w_qkv[0].astype(BF16)
    ones_v = jnp.ones((kd,), F32)
    gain_lat = jnp.concatenate([jnp.tile(g_q[0] * scale, qd // HEAD_DIM),
                                jnp.tile(g_k[0], kd // HEAD_DIM), ones_v]).reshape(1, -1)
    gain_ctx = jnp.concatenate([jnp.tile(g_k[0], kd // HEAD_DIM), ones_v]).reshape(1, -1)
    tmq = _tile(S, tiles["qkv"])
    qkv = _qkv_proj(x2, g_norm1[0], modv, 0, lambda i: i // (S // tmq), wq, gain_lat, qd + kd,
                    _rope_tables(S), tmq)
    kvc = _qkv_proj(ctx.reshape(B * C, D), g_norm1[0], modv, 0, lambda i: B, wq[:, qd:], gain_ctx,
                    kd, None, C)
    o = _attention(qkv, kvc, sink[0], B, S, _tile(S, tiles["attn"]))
    mixed = _oproj(o, x2, w_o[0].astype(BF16), g_norm2[0], modv, 0, S, wr,
                   _tile(S, tiles["oproj"]))
    moe_w = (modv, b_router, w_gate, w_up, w_down, ws_gate, ws_up, ws_down, tiles)
    x2 = _moe_block(mixed, 0, S, *moe_w)

    mixed = _pool_mixer(x2, g_norm1[1], g_norm2[1], modv, 1, B, S, w_pool[0].astype(BF16),
                              pool_scale[0], wr, _tile(S, tiles["pool"]))
    x2 = _moe_block(mixed, 1, S, *moe_w)
    return x2.reshape(B, S, D)
```

```python
import functools

import jax
import jax.numpy as jnp
from jax import lax
from jax.experimental import pallas as pl
from jax.experimental.pallas import tpu as pltpu

HEAD_DIM = 128
GQA_GROUP = 4
GRID_W = 64
WINDOW = 128
ROPE_BASE = 10000.0
POOL_WINDOWS = (2, 4, 8, 16)
POOL_HALO = 8
N_EXPERT_GROUPS = 8
TOPK_GROUPS = 4
TOP_K = 8
ROUTED_SCALE = 2.5
EPS = 1e-6
NEG = -1e30

LANES = 128
SUBLANES = 8
VMEM_LIMIT = 52 * 1024 * 1024
VMEM_LIMIT_EXPERTS = 58 * 1024 * 1024
F32 = jnp.float32
BF16 = jnp.bfloat16


def _cparams(sem):
    return pltpu.CompilerParams(dimension_semantics=sem, vmem_limit_bytes=VMEM_LIMIT)


def _tile(n, want):
    t = min(n, want)
    while n % t:
        t //= 2
    return t


def _silu(v):
    return v * jax.nn.sigmoid(v)


def _rms_mod(x, g, shift, scale):
    ms = jnp.mean(x * x, axis=-1, keepdims=True)
    return (x * lax.rsqrt(ms + EPS) * g) * (1.0 + scale) + shift


def _adaln_kernel(c_ref, w_ref, b_ref, o_ref):
    a = _silu(c_ref[...])
    o_ref[...] = jnp.dot(a.astype(BF16), w_ref[...].astype(BF16),
                         preferred_element_type=F32) + b_ref[...]


def _adaln(cond8, w_ada, b_ada):
    L, D, D6 = w_ada.shape
    tn = _tile(D6, 1024)
    return pl.pallas_call(
        _adaln_kernel,
        out_shape=jax.ShapeDtypeStruct((L, SUBLANES, D6), F32),
        grid=(L, D6 // tn),
        in_specs=[
            pl.BlockSpec((SUBLANES, D), lambda l, j: (0, 0)),
            pl.BlockSpec((None, D, tn), lambda l, j: (l, 0, j)),
            pl.BlockSpec((None, 1, tn), lambda l, j: (l, 0, j)),
        ],
        out_specs=pl.BlockSpec((None, SUBLANES, tn), lambda l, j: (l, 0, j)),
        compiler_params=_cparams(("parallel", "parallel")),
        name="adaln",
    )(cond8, w_ada, b_ada.reshape(L, 1, D6))


def _rope(a, cos, sin_signed):
    lane = lax.broadcasted_iota(jnp.int32, a.shape, 1)
    partner = jnp.where((lane & 32) == 0, pltpu.roll(a, 96, 1), pltpu.roll(a, 32, 1))
    return a * cos + partner * sin_signed


def _qkv_kernel(*refs, n_norm_tiles, rope):
    if rope:
        x_ref, g1_ref, sh_ref, sc_ref, w_ref, gain_ref, cos_ref, sin_ref, o_ref, h_scr = refs
    else:
        x_ref, g1_ref, sh_ref, sc_ref, w_ref, gain_ref, o_ref, h_scr = refs
    j = pl.program_id(1)

    @pl.when(j == 0)
    def _():
        h_scr[...] = _rms_mod(x_ref[...], g1_ref[...], sh_ref[...], sc_ref[...]).astype(BF16)

    acc = jnp.dot(h_scr[...], w_ref[...], preferred_element_type=F32)

    @pl.when(j < n_norm_tiles)
    def _():
        for hh in range(acc.shape[1] // HEAD_DIM):
            sl = slice(hh * HEAD_DIM, (hh + 1) * HEAD_DIM)
            a = acc[:, sl]
            ms = jnp.mean(a * a, axis=-1, keepdims=True)
            a = a * lax.rsqrt(ms + EPS) * gain_ref[:, sl]
            if rope:
                a = _rope(a, cos_ref[...], sin_ref[...])
            o_ref[:, sl] = a.astype(o_ref.dtype)

    @pl.when(j >= n_norm_tiles)
    def _():
        o_ref[...] = acc.astype(o_ref.dtype)


def _qkv_proj(x2, g1, modv, layer, mod_row_fn, w_bf16, gain, n_norm_cols, rope_tabs, tm):
    N, D = x2.shape
    ncols = w_bf16.shape[1]
    tn = _tile(n_norm_cols, 512)
    assert ncols % tn == 0 and tn % HEAD_DIM == 0
    rope = rope_tabs is not None
    in_specs = [
        pl.BlockSpec((tm, D), lambda i, j: (i, 0)),
        pl.BlockSpec((1, D), lambda i, j: (0, 0)),
        pl.BlockSpec((None, None, None, 1, D), lambda i, j: (layer, 0, mod_row_fn(i), 0, 0)),
        pl.BlockSpec((None, None, None, 1, D), lambda i, j: (layer, 1, mod_row_fn(i), 0, 0)),
        pl.BlockSpec((D, tn), lambda i, j: (0, j)),
        pl.BlockSpec((1, tn), lambda i, j: (0, j)),
    ]
    args = [x2, g1.reshape(1, D), modv, modv, w_bf16, gain]
    if rope:
        cos, sin = rope_tabs
        tpb = cos.shape[0] // tm
        in_specs += [pl.BlockSpec((tm, HEAD_DIM), lambda i, j: (i % tpb, 0))] * 2
        args += [cos, sin]
    return pl.pallas_call(
        functools.partial(_qkv_kernel, n_norm_tiles=n_norm_cols // tn, rope=rope),
        out_shape=jax.ShapeDtypeStruct((N, ncols), BF16),
        grid=(N // tm, ncols // tn),
        in_specs=in_specs,
        out_specs=pl.BlockSpec((tm, tn), lambda i, j: (i, j)),
        scratch_shapes=[pltpu.VMEM((tm, D), BF16)],
        compiler_params=_cparams(("parallel", "arbitrary")),
        name="qkv_rope" if rope else "ctx_kv",
    )(*args)


def _rope_tables(S):
    rows = S // GRID_W
    row = jnp.repeat(jnp.arange(rows), GRID_W).astype(F32)
    col = jnp.tile(jnp.arange(GRID_W), rows).astype(F32)
    nf = HEAD_DIM // 4
    inv = ROPE_BASE ** (-jnp.arange(nf, dtype=F32) / nf)
    ar, ac = row[:, None] * inv, col[:, None] * inv
    cos = jnp.concatenate([jnp.cos(ar), jnp.cos(ar), jnp.cos(ac), jnp.cos(ac)], axis=1)
    sin = jnp.concatenate([-jnp.sin(ar), jnp.sin(ar), -jnp.sin(ac), jnp.sin(ac)], axis=1)
    return cos, sin


def _attn_kernel(sink_ref, bias_ref, q_ref, kp_ref, kc_ref, kn_ref, vp_ref, vc_ref, vn_ref, kx_ref,
                 vx_ref, o_ref, k_scr, v_scr, *, tq):
    kh = pl.program_id(2)
    nloc = tq + 2 * WINDOW
    nctx = kx_ref.shape[0]
    for scr, prev, cur, nxt, cx in ((k_scr, kp_ref, kc_ref, kn_ref, kx_ref),
                                    (v_scr, vp_ref, vc_ref, vn_ref, vx_ref)):
        scr[0:WINDOW, :] = prev[...]
        scr[WINDOW:WINDOW + tq, :] = cur[...]
        scr[WINDOW + tq:nloc, :] = nxt[...]
        scr[nloc:nloc + nctx, :] = cx[...]
    kall = k_scr[...]
    vall = v_scr[...]
    for g in range(GQA_GROUP):
        sl = slice(g * HEAD_DIM, (g + 1) * HEAD_DIM)
        s = lax.dot_general(q_ref[:, sl], kall, (((1,), (1,)), ((), ())),
                            preferred_element_type=F32)
        s = s + bias_ref[...]
        sink = sink_ref[kh * GQA_GROUP + g]
        m = jnp.maximum(jnp.max(s, axis=-1, keepdims=True), sink)
        p = jnp.exp(s - m)
        denom = jnp.sum(p, axis=-1, keepdims=True) + jnp.exp(sink - m)
        o = jnp.dot(p.astype(BF16), vall, preferred_element_type=F32)
        o_ref[:, sl] = (o / denom).astype(o_ref.dtype)


def _attention(qkv, kvc, sink, B, S, tq):
    N, ncols = qkv.shape
    C = kvc.shape[0] // B
    KV = kvc.shape[1] // (2 * HEAD_DIM)
    H = KV * GQA_GROUP
    assert ncols == (H + 2 * KV) * HEAD_DIM and tq % WINDOW == 0 and S % tq == 0
    tpb, wpb, wpt = S // tq, S // WINDOW, tq // WINDOW
    gw = GQA_GROUP * HEAD_DIM

    def prev_map(col0):
        return lambda b, i, k: (jnp.maximum(b * wpb + i * wpt - 1, b * wpb), col0 + k)

    def cur_map(col0):
        return lambda b, i, k: (b * tpb + i, col0 + k)

    def next_map(col0):
        return lambda b, i, k: (jnp.minimum(b * wpb + (i + 1) * wpt, (b + 1) * wpb - 1), col0 + k)

    halo = lambda m: pl.BlockSpec((WINDOW, HEAD_DIM), m)
    cur = lambda m: pl.BlockSpec((tq, HEAD_DIM), m)
    nkeys = tq + 2 * WINDOW + C
    r = jnp.arange(tq)[:, None]
    c = jnp.arange(nkeys)[None, :]
    band = jnp.abs(r - (c - WINDOW)) <= WINDOW
    variants = [band & ((c >= WINDOW) | (not first)) & ((c < tq + WINDOW) | (not last))
                for last in (False, True) for first in (False, True)]
    bias = jnp.where(jnp.stack(variants) | (c >= tq + 2 * WINDOW), 0.0, NEG).astype(F32)
    in_specs = [
        pl.BlockSpec(memory_space=pltpu.SMEM),
        pl.BlockSpec((None, tq, nkeys),
                     lambda b, i, k: ((i == 0).astype(jnp.int32)
                                      + 2 * (i == tpb - 1).astype(jnp.int32), 0, 0)),
        pl.BlockSpec((tq, gw), lambda b, i, k: (b * tpb + i, k)),
        halo(prev_map(H)), cur(cur_map(H)), halo(next_map(H)),
        halo(prev_map(H + KV)), cur(cur_map(H + KV)), halo(next_map(H + KV)),
        pl.BlockSpec((C, HEAD_DIM), lambda b, i, k: (b, k)),
        pl.BlockSpec((C, HEAD_DIM), lambda b, i, k: (b, KV + k)),
    ]
    return pl.pallas_call(
        functools.partial(_attn_kernel, tq=tq),
        out_shape=jax.ShapeDtypeStruct((N, H * HEAD_DIM), BF16),
        grid=(B, tpb, KV),
        in_specs=in_specs,
        out_specs=pl.BlockSpec((tq, gw), lambda b, i, k: (b * tpb + i, k)),
        scratch_shapes=[pltpu.VMEM((nkeys, HEAD_DIM), BF16), pltpu.VMEM((nkeys, HEAD_DIM), BF16)],
        compiler_params=_cparams(("parallel", "parallel", "parallel")),
        name="band_attn",
    )(sink, bias, qkv, qkv, qkv, qkv, qkv, qkv, qkv, kvc, kvc)


def _split_bf16(v):
    hi = v.astype(BF16)
    return hi, (v - hi.astype(F32)).astype(BF16)


def _post_mixer(x1, g2_ref, sh2_ref, sc2_ref, wr_ref, x1_ref, h2_ref, h2r_ref, lg_ref):
    x1_ref[...] = x1
    h2 = _rms_mod(x1, g2_ref[...], sh2_ref[...], sc2_ref[...])
    h2_ref[...] = h2.astype(h2_ref.dtype)
    h2r_ref[...] = h2.reshape(h2r_ref.shape)
    hi, lo = _split_bf16(h2)
    dot = functools.partial(jnp.dot, preferred_element_type=F32)
    lg_ref[...] = dot(hi, wr_ref[0]) + (dot(hi, wr_ref[1]) + dot(lo, wr_ref[0]))


def _oproj_kernel(o_ref, x_ref, wo_ref, ga1_ref, g2_ref, sh2_ref, sc2_ref, wr_ref, *out_refs):
    y = jnp.dot(o_ref[...], wo_ref[...], preferred_element_type=F32)
    _post_mixer(x_ref[...] + ga1_ref[...] * y, g2_ref, sh2_ref, sc2_ref, wr_ref, *out_refs)


def _mod_spec(layer, chunk, row_fn, D, nargs):
    if nargs == 1:
        return pl.BlockSpec((None, None, None, 1, D), lambda i: (layer, chunk, row_fn(i), 0, 0))
    return pl.BlockSpec((None, None, None, 1, D), lambda b, i: (layer, chunk, b, 0, 0))


def _post_mixer_out(N, D, tm, row_map):
    shapes = (jax.ShapeDtypeStruct((N, D), F32), jax.ShapeDtypeStruct((N, D), BF16),
              jax.ShapeDtypeStruct((N, 1, D), F32), jax.ShapeDtypeStruct((N, LANES), F32))
    specs = (pl.BlockSpec((tm, D), lambda *a: (row_map(*a), 0)),
             pl.BlockSpec((tm, D), lambda *a: (row_map(*a), 0)),
             pl.BlockSpec((tm, 1, D), lambda *a: (row_map(*a), 0, 0)),
             pl.BlockSpec((tm, LANES), lambda *a: (row_map(*a), 0)))
    return shapes, specs


def _router_weights(w_router):
    L, D, E = w_router.shape
    assert E <= LANES
    w = jnp.pad(w_router, ((0, 0), (0, 0), (0, LANES - E)))
    hi = w.astype(BF16)
    lo = (w - hi.astype(F32)).astype(BF16)
    return jnp.stack([hi, lo], axis=1)


def _oproj(o, x2, wo_bf16, g2, modv, layer, S, wr, tm):
    N, D = x2.shape
    qd = o.shape[1]
    tpb = S // tm
    row = lambda i: i // tpb
    shapes, specs = _post_mixer_out(N, D, tm, lambda i: i)
    return pl.pallas_call(
        _oproj_kernel,
        out_shape=shapes,
        grid=(N // tm,),
        in_specs=[
            pl.BlockSpec((tm, qd), lambda i: (i, 0)),
            pl.BlockSpec((tm, D), lambda i: (i, 0)),
            pl.BlockSpec((qd, D), lambda i: (0, 0)),
            _mod_spec(layer, 2, row, D, 1),
            pl.BlockSpec((1, D), lambda i: (0, 0)),
            _mod_spec(layer, 3, row, D, 1),
            _mod_spec(layer, 4, row, D, 1),
            pl.BlockSpec((None, 2, D, LANES), lambda i: (layer, 0, 0, 0)),
        ],
        out_specs=specs,
        compiler_params=_cparams(("parallel",)),
        name="oproj_norm_router",
    )(o, x2, wo_bf16, modv, g2.reshape(1, D), modv, modv, wr)


def _pool_kernel(xp_ref, xc_ref, xn_ref, g1_ref, sh1_ref, sc1_ref, wp_ref, ps_ref, ga1_ref,
                 g2_ref, sh2_ref, sc2_ref, wr_ref, x1_ref, h2_ref, h2r_ref, lg_ref, halo_scr,
                 *, ts, seq):
    i = pl.program_id(1)
    pos0 = i * ts
    norm = lambda v: _rms_mod(v, g1_ref[...], sh1_ref[...], sc1_ref[...])
    xc = xc_ref[...]
    hc = norm(xc)
    hp = jnp.where(pos0 > 0, norm(xp_ref[...]), 0.0)
    hn = jnp.where(pos0 + ts < seq, norm(xn_ref[...]), 0.0)
    halo_scr[...] = jnp.zeros_like(halo_scr)
    halo_scr[0:POOL_HALO, :] = hp
    halo_scr[POOL_HALO:2 * POOL_HALO, :] = hn
    hc16 = hc.astype(BF16)
    halo16 = halo_scr[...].astype(BF16)

    r_c = lax.broadcasted_iota(jnp.int32, (ts, ts), 0)
    c_c = lax.broadcasted_iota(jnp.int32, (ts, ts), 1)
    r_h = lax.broadcasted_iota(jnp.int32, (ts, LANES), 0)
    c_h = lax.broadcasted_iota(jnp.int32, (ts, LANES), 1)
    p_h = jnp.where(c_h < POOL_HALO, c_h - POOL_HALO, ts + c_h - POOL_HALO)
    in_halo = c_h < 2 * POOL_HALO
    pos = pos0 + lax.broadcasted_iota(jnp.int32, (ts, 1), 0)
    gw = xc.shape[1] // len(POOL_WINDOWS)
    ys = []
    for g, w in enumerate(POOL_WINDOWS):
        lo, hi = w // 2, w - w // 2
        sl = slice(g * gw, (g + 1) * gw)
        band_c = ((c_c >= r_c - lo) & (c_c < r_c + hi)).astype(BF16)
        band_h = (in_halo & (p_h >= r_h - lo) & (p_h < r_h + hi)).astype(BF16)
        tot = (jnp.dot(band_c, hc16[:, sl], preferred_element_type=F32)
               + jnp.dot(band_h, halo16[:, sl], preferred_element_type=F32))
        cnt = (jnp.minimum(pos + hi, seq) - jnp.maximum(pos - lo, 0)).astype(F32)
        d = tot / cnt - hc[:, sl]
        ys.append(jnp.dot(d.astype(BF16), wp_ref[g], preferred_element_type=F32))
    y = jnp.concatenate(ys, axis=1) * ps_ref[...]
    _post_mixer(xc + ga1_ref[...] * y, g2_ref, sh2_ref, sc2_ref, wr_ref,
                x1_ref, h2_ref, h2r_ref, lg_ref)


def _pool_mixer(x2, g1, g2, modv, layer, B, S, wp_bf16, pool_scale, wr, ts):
    N, D = x2.shape
    G, gw, _ = wp_bf16.shape
    tpb = S // ts
    hpt, hpb = ts // POOL_HALO, S // POOL_HALO
    shapes, specs = _post_mixer_out(N, D, ts, lambda b, i: b * tpb + i)
    mod = lambda chunk: _mod_spec(layer, chunk, None, D, 2)
    vec = lambda: pl.BlockSpec((1, D), lambda b, i: (0, 0))
    return pl.pallas_call(
        functools.partial(_pool_kernel, ts=ts, seq=S),
        out_shape=shapes,
        grid=(B, tpb),
        in_specs=[
            pl.BlockSpec((POOL_HALO, D), lambda b, i: (jnp.maximum(b * hpb + i * hpt - 1, b * hpb), 0)),
            pl.BlockSpec((ts, D), lambda b, i: (b * tpb + i, 0)),
            pl.BlockSpec((POOL_HALO, D),
                         lambda b, i: (jnp.minimum(b * hpb + (i + 1) * hpt, (b + 1) * hpb - 1), 0)),
            vec(), mod(0), mod(1),
            pl.BlockSpec((G, gw, gw), lambda b, i: (0, 0, 0)),
            vec(), mod(2), vec(), mod(3), mod(4),
            pl.BlockSpec((None, 2, D, LANES), lambda b, i: (layer, 0, 0, 0)),
        ],
        out_specs=specs,
        scratch_shapes=[pltpu.VMEM((LANES, D), F32)],
        compiler_params=_cparams(("parallel", "parallel")),
        name="pool_norm_router",
    )(x2, x2, x2, g1.reshape(1, D), modv, modv, wp_bf16, pool_scale.reshape(1, D), modv,
      g2.reshape(1, D), modv, modv, wr)


def _first_argmax(v, idx, big):
    m = jnp.max(v, axis=0, keepdims=True)
    first = jnp.min(jnp.where(v == m, idx, big), axis=0, keepdims=True)
    return m, first


def _route_kernel(lg_ref, b_ref, eidx_ref, gate_ref):
    E = b_ref.shape[0]
    s = jax.nn.sigmoid(lg_ref[...].T[0:E, :])
    sb = s + b_ref[...]
    tn = s.shape[1]
    per = E // N_EXPERT_GROUPS
    sub = lax.broadcasted_iota(jnp.int32, (per, tn), 0).astype(F32)
    gid = lax.broadcasted_iota(jnp.int32, (N_EXPERT_GROUPS, tn), 0).astype(F32)
    gs = jnp.zeros((N_EXPERT_GROUPS, tn), F32)
    for g in range(N_EXPERT_GROUPS):
        blk = sb[g * per:(g + 1) * per, :]
        m1, i1 = _first_argmax(blk, sub, float(per))
        m2 = jnp.max(jnp.where(sub == i1, -jnp.inf, blk), axis=0, keepdims=True)
        gs = jnp.where(gid == float(g), m1 + m2, gs)
    gsel = jnp.zeros_like(gs)
    for _ in range(TOPK_GROUPS):
        _, gi = _first_argmax(gs, gid, float(N_EXPERT_GROUPS))
        hit = gid == gi
        gsel = jnp.where(hit, 1.0, gsel)
        gs = jnp.where(hit, -jnp.inf, gs)
    cur = jnp.concatenate(
        [jnp.where(gsel[g:g + 1, :] > 0.0, sb[g * per:(g + 1) * per, :], NEG)
         for g in range(N_EXPERT_GROUPS)], axis=0)
    eid = lax.broadcasted_iota(jnp.int32, (E, tn), 0).astype(F32)
    sels = []
    for k in range(TOP_K):
        _, ei = _first_argmax(cur, eid, float(E))
        hit = eid == ei
        eidx_ref[k:k + 1, :] = ei.astype(jnp.int32)
        sels.append(jnp.sum(jnp.where(hit, s, 0.0), axis=0, keepdims=True))
        cur = jnp.where(hit, -jnp.inf, cur)
    denom = sels[0]
    for k in range(1, TOP_K):
        denom = denom + sels[k]
    for k in range(TOP_K):
        gate_ref[k:k + 1, :] = sels[k] / denom * ROUTED_SCALE


def _route(lg, b_router, tn):
    N = lg.shape[0]
    E = b_router.shape[0]
    return pl.pallas_call(
        _route_kernel,
        out_shape=(jax.ShapeDtypeStruct((TOP_K, N), jnp.int32),
                   jax.ShapeDtypeStruct((TOP_K, N), F32)),
        grid=(N // tn,),
        in_specs=[pl.BlockSpec((tn, LANES), lambda i: (i, 0)),
                  pl.BlockSpec((E, 1), lambda i: (0, 0))],
        out_specs=(pl.BlockSpec((TOP_K, tn), lambda i: (0, i)),
                   pl.BlockSpec((TOP_K, tn), lambda i: (0, i))),
        compiler_params=_cparams(("parallel",)),
        name="route_topk",
    )(lg, b_router.reshape(E, 1))


def _dispatch_plan(eidx, gates, E, tm, n_tiles_max):
    K, N = eidx.shape
    P = N * K
    flat_e = eidx.T.reshape(P)
    flat_g = gates.T.reshape(P)
    perm = jnp.argsort(flat_e, stable=True).astype(jnp.int32)
    counts = jnp.sum((flat_e[:, None] == jnp.arange(E, dtype=jnp.int32)[None, :]).astype(jnp.int32),
                     axis=0)
    tiles_e = (counts + tm - 1) // tm
    tile_end = jnp.cumsum(tiles_e)
    tile_start = tile_end - tiles_e
    n_tiles = tile_end[-1]
    cs = jnp.cumsum(counts) - counts
    tj = jnp.arange(n_tiles_max, dtype=jnp.int32)
    te = jnp.minimum(jnp.sum((tj[:, None] >= tile_end[None, :]).astype(jnp.int32), axis=1), E - 1)
    te = jnp.where(tj < n_tiles, te, te[jnp.maximum(n_tiles - 1, 0)])
    first = (tj - tile_start[te]) * tm
    nvalid = jnp.where(tj < n_tiles, jnp.clip(counts[te] - first, 0, tm), 0)
    r = jnp.arange(tm, dtype=jnp.int32)[None, :]
    valid = r < nvalid[:, None]
    pair = jnp.where(valid, perm[jnp.clip((cs[te] + first)[:, None] + r, 0, P - 1)], 0)
    gate = jnp.where(valid, flat_g[pair], 0.0)
    tok, k = pair // K, pair % K
    i32 = lambda v: v.astype(jnp.int32)
    shape3 = (n_tiles_max, 1, tm)
    return (i32(te), i32(n_tiles).reshape(1), i32(nvalid), i32(tok).reshape(shape3),
            i32(k * N + tok).reshape(shape3), gate.reshape(n_tiles_max * tm, 1))


ROW_BUFS = 3


def _moe_kernel(te_ref, nt_ref, nv_ref, src0_ref, src1_ref, srcn_ref, dstp_ref, gate_ref,
                wg_ref, wu_ref, wd_ref, h_hbm, out_hbm,
                xbuf, ybuf, wg_s, wu_s, wd_s, gsem, ssem, *, tm):
    i = pl.program_id(0)
    nt = nt_ref[0]
    slot = lax.rem(i, ROW_BUFS)
    other = lax.rem(i + 2, ROW_BUFS)
    tile_nv = lambda j: nv_ref[jnp.maximum(j, 0)]
    nvp = tile_nv(i - 1)

    def gather_row(t, r, s):
        pltpu.make_async_copy(h_hbm.at[t], xbuf.at[s, pl.ds(r, 1), :], gsem.at[s]).start()

    def scatter_row(d, r, s):
        pltpu.make_async_copy(ybuf.at[s, pl.ds(r, 1), :], out_hbm.at[d], ssem.at[s]).start()

    def issue(gather_idx, gs, scatter_idx, ss, part=0, parts=1):
        for r in range(part * tm // parts, (part + 1) * tm // parts):
            if gather_idx is not None:
                gather_row(gather_idx[0, 0, r], r, gs)
            if scatter_idx is not None:
                scatter_row(scatter_idx[0, 0, r], r, ss)

    def gather_wait(s):
        pltpu.make_async_copy(ybuf.at[s], xbuf.at[s], gsem.at[s]).wait()

    def scatter_wait(s, nv):
        p = tm
        while p >= 1:
            @pl.when((nv & p) != 0)
            def _(p=p):
                pltpu.make_async_copy(ybuf.at[s, pl.ds(0, p), :], xbuf.at[s, pl.ds(0, p), :],
                                      ssem.at[s]).wait()
            p //= 2

    @pl.when(i == 0)
    def _():
        issue(src0_ref, 0, None, None)
        issue(src1_ref, 1, None, None)

    @pl.when((i >= 1) & (i <= nt) & (nvp < tm))
    def _():
        def body(r, carry):
            scatter_row(dstp_ref[0, 0, r], r, other)
            return carry
        lax.fori_loop(0, nvp, body, 0)

    @pl.when(i < nt)
    def _():
        gather_wait(slot)

        @pl.when(i >= ROW_BUFS)
        def _():
            scatter_wait(slot, tile_nv(i - ROW_BUFS))

        @pl.when((i == 0) | (te_ref[i] != te_ref[jnp.maximum(i - 1, 0)]))
        def _():
            wg_s[...] = wg_ref[...].astype(BF16)
            wu_s[...] = wu_ref[...].astype(BF16)
            wd_s[...] = wd_ref[...].astype(BF16)

        def step(scatter_prev):
            copies = functools.partial(issue, srcn_ref, other,
                                       dstp_ref if scatter_prev else None, other, parts=3)
            x = xbuf[slot].astype(BF16)
            g = jnp.dot(x, wg_s[...], preferred_element_type=F32)
            copies(part=0)
            u = jnp.dot(x, wu_s[...], preferred_element_type=F32)
            copies(part=1)
            hid = (_silu(g) * u).astype(BF16)
            y = jnp.dot(hid, wd_s[...], preferred_element_type=F32) * gate_ref[...]
            copies(part=2)
            ybuf[slot] = y

        full_prev = (i >= 1) & (nvp == tm)

        @pl.when(full_prev)
        def _():
            step(True)

        @pl.when(jnp.logical_not(full_prev))
        def _():
            step(False)

    @pl.when(i == nt)
    def _():
        @pl.when(nvp == tm)
        def _():
            issue(None, None, dstp_ref, other)
        gather_wait(slot)
        gather_wait(lax.rem(i + 1, ROW_BUFS))
        for back in range(ROW_BUFS, 0, -1):
            @pl.when(nt >= back)
            def _(back=back):
                scatter_wait(lax.rem(nt - back + ROW_BUFS, ROW_BUFS), tile_nv(nt - back))


def _experts(h2r, plan, w_gate, w_up, w_down, layer, tm):
    te, nt, nv, src, dst, gate = plan
    N, _, D = h2r.shape
    _, E, _, F = w_gate.shape
    T = src.shape[0]
    assert tm & (tm - 1) == 0 and N >= tm and N * TOP_K >= tm
    smem_blk = lambda m: pl.BlockSpec((1, 1, tm), m, memory_space=pltpu.SMEM)
    clamp = lambda v: jnp.clip(v, 0, T - 1)
    wsel = lambda i, te, nt, nv: (layer, te[clamp(i)], 0, 0)
    grid_spec = pltpu.PrefetchScalarGridSpec(
        num_scalar_prefetch=3,
        grid=(T + 1,),
        in_specs=[
            smem_blk(lambda i, te, nt, nv: (0, 0, 0)),
            smem_blk(lambda i, te, nt, nv: (clamp(1), 0, 0)),
            smem_blk(lambda i, te, nt, nv: (clamp(i + 2), 0, 0)),
            smem_blk(lambda i, te, nt, nv: (clamp(i - 1), 0, 0)),
            pl.BlockSpec((tm, 1), lambda i, te, nt, nv: (clamp(i), 0)),
            pl.BlockSpec((None, None, D, F), wsel),
            pl.BlockSpec((None, None, D, F), wsel),
            pl.BlockSpec((None, None, F, D), wsel),
            pl.BlockSpec(memory_space=pl.ANY),
        ],
        out_specs=pl.BlockSpec(memory_space=pl.ANY),
        scratch_shapes=[
            pltpu.VMEM((ROW_BUFS, tm, D), F32),
            pltpu.VMEM((ROW_BUFS, tm, D), F32),
            pltpu.VMEM((D, F), BF16),
            pltpu.VMEM((D, F), BF16),
            pltpu.VMEM((F, D), BF16),
            pltpu.SemaphoreType.DMA((ROW_BUFS,)),
            pltpu.SemaphoreType.DMA((ROW_BUFS,)),
        ],
    )
    return pl.pallas_call(
        functools.partial(_moe_kernel, tm=tm),
        out_shape=jax.ShapeDtypeStruct((TOP_K * N, 1, D), F32),
        grid_spec=grid_spec,
        compiler_params=pltpu.CompilerParams(dimension_semantics=("arbitrary",),
                                             vmem_limit_bytes=VMEM_LIMIT_EXPERTS),
        name="moe_experts",
    )(te, nt, nv, src, src, src, dst, gate, w_gate, w_up, w_down, h2r)


def _combine_kernel(h2_ref, x1_ref, *refs):
    po_refs = refs[:TOP_K]
    wsg_ref, wsu_ref, wsd_ref, ga2_ref, o_ref, row_scr = refs[TOP_K:]
    h = h2_ref[...]
    g = jnp.dot(h, wsg_ref[...], preferred_element_type=F32)
    u = jnp.dot(h, wsu_ref[...], preferred_element_type=F32)
    acc = jnp.dot((_silu(g) * u).astype(BF16), wsd_ref[...], preferred_element_type=F32)
    for po_ref in po_refs:
        row_scr[...] = po_ref[...].reshape(row_scr.shape)
        acc = acc + row_scr[...]
    o_ref[...] = x1_ref[...] + ga2_ref[...] * acc


def _combine(h2, x1, pair_out, wsg, wsu, wsd, modv, layer, S, tm):
    N, D = x1.shape
    F = wsg.shape[1]
    tpb = S // tm
    nblk = N // tm
    slot_spec = lambda k: pl.BlockSpec((tm, 1, D), lambda i: (k * nblk + i, 0, 0))
    return pl.pallas_call(
        _combine_kernel,
        out_shape=jax.ShapeDtypeStruct((N, D), F32),
        grid=(nblk,),
        in_specs=[
            pl.BlockSpec((tm, D), lambda i: (i, 0)),
            pl.BlockSpec((tm, D), lambda i: (i, 0)),
            *[slot_spec(k) for k in range(TOP_K)],
            pl.BlockSpec((D, F), lambda i: (0, 0)),
            pl.BlockSpec((D, F), lambda i: (0, 0)),
            pl.BlockSpec((F, D), lambda i: (0, 0)),
            _mod_spec(layer, 5, lambda i: i // tpb, D, 1),
        ],
        out_specs=pl.BlockSpec((tm, D), lambda i: (i, 0)),
        scratch_shapes=[pltpu.VMEM((tm, D), F32)],
        compiler_params=_cparams(("parallel",)),
        name="shared_combine",
    )(h2, x1, *([pair_out] * TOP_K), wsg, wsu, wsd, modv)


def _moe_block(mixed, layer, S, modv, b_router, w_gate, w_up, w_down, ws_gate, ws_up, ws_down, tiles):
    x1, h2, h2r, lgt = mixed
    N, D = x1.shape
    E = b_router.shape[1]
    tm = tiles["expert"]
    eidx, gates = _route(lgt, b_router[layer], _tile(N, tiles["route"]))
    plan = _dispatch_plan(eidx, gates, E, tm, (N * TOP_K) // tm + E)
    pair_out = _experts(h2r, plan, w_gate, w_up, w_down, layer, tm)
    return _combine(h2, x1, pair_out, ws_gate[layer].astype(BF16), ws_up[layer].astype(BF16),
                    ws_down[layer].astype(BF16), modv, layer, S, _tile(S, tiles["combine"]))


DEFAULT_TILES = dict(qkv=512, attn=256, oproj=256, pool=256, route=512, expert=256, combine=128)


def kernel(x, c, ctx, c_ctx, w_ada, b_ada, g_norm1, g_norm2, w_qkv, g_q, g_k, sink, w_o, w_pool,
           pool_scale, w_router, b_router, w_gate, w_up, w_down, ws_gate, ws_up, ws_down):
    return _forward(DEFAULT_TILES, x, c, ctx, c_ctx, w_ada, b_ada, g_norm1, g_norm2, w_qkv, g_q, g_k,
                    sink, w_o, w_pool, pool_scale, w_router, b_router, w_gate, w_up, w_down,
                    ws_gate, ws_up, ws_down)


def _forward(tiles, x, c, ctx, c_ctx, w_ada, b_ada, g_norm1, g_norm2, w_qkv, g_q, g_k, sink, w_o,
             w_pool, pool_scale, w_router, b_router, w_gate, w_up, w_down, ws_gate, ws_up, ws_down):
    B, S, D = x.shape
    C = ctx.shape[1]
    L = w_ada.shape[0]
    assert L == 2 and B + 1 <= SUBLANES, "two-layer trunk: attention layer then pooling layer"
    N = B * S
    qd = w_o.shape[1]
    kd = (w_qkv.shape[2] - qd) // 2

    cond8 = jnp.zeros((SUBLANES, D), F32).at[:B].set(c).at[B].set(c_ctx)
    mod = _adaln(cond8, w_ada, b_ada)
    modv = mod.reshape(L, SUBLANES, 6, D).transpose(0, 2, 1, 3).reshape(L, 6, SUBLANES, 1, D)
    wr = _router_weights(w_router)
    x2 = x.reshape(N, D)

    scale = HEAD_DIM ** -0.5
    wq = w_qkv[0].astype(BF16)
    ones_v = jnp.ones((kd,), F32)
    gain_lat = jnp.concatenate([jnp.tile(g_q[0] * scale, qd // HEAD_DIM),
                                jnp.tile(g_k[0], kd // HEAD_DIM), ones_v]).reshape(1, -1)
    gain_ctx = jnp.concatenate([jnp.tile(g_k[0], kd // HEAD_DIM), ones_v]).reshape(1, -1)
    tmq = _tile(S, tiles["qkv"])
    qkv = _qkv_proj(x2, g_norm1[0], modv, 0, lambda i: i // (S // tmq), wq, gain_lat, qd + kd,
                    _rope_tables(S), tmq)
    kvc = _qkv_proj(ctx.reshape(B * C, D), g_norm1[0], modv, 0, lambda i: B, wq[:, qd:], gain_ctx,
                    kd, None, C)
    o = _attention(qkv, kvc, sink[0], B, S, _tile(S, tiles["attn"]))
    mixed = _oproj(o, x2, w_o[0].astype(BF16), g_norm2[0], modv, 0, S, wr,
                   _tile(S, tiles["oproj"]))
    moe_w = (modv, b_router, w_gate, w_up, w_down, ws_gate, ws_up, ws_down, tiles)
    x2 = _moe_block(mixed, 0, S, *moe_w)

    mixed = _pool_mixer(x2, g_norm1[1], g_norm2[1], modv, 1, B, S, w_pool[0].astype(BF16),
                              pool_scale[0], wr, _tile(S, tiles["pool"]))
    x2 = _moe_block(mixed, 1, S, *moe_w)
    return x2.reshape(B, S, D)
```

```python
import functools

import jax
import jax.numpy as jnp
from jax import lax
from jax.experimental import pallas as pl
from jax.experimental.pallas import tpu as pltpu

HEAD_DIM = 128
GQA_GROUP = 4
GRID_W = 64
WINDOW = 128
ROPE_BASE = 10000.0
POOL_WINDOWS = (2, 4, 8, 16)
POOL_HALO = 8
N_EXPERT_GROUPS = 8
TOPK_GROUPS = 4
TOP_K = 8
ROUTED_SCALE = 2.5
EPS = 1e-6
NEG = -1e30

LANES = 128
SUBLANES = 8
VMEM_LIMIT = 52 * 1024 * 1024
VMEM_LIMIT_EXPERTS = 58 * 1024 * 1024
F32 = jnp.float32
BF16 = jnp.bfloat16


def _cparams(sem):
    return pltpu.CompilerParams(dimension_semantics=sem, vmem_limit_bytes=VMEM_LIMIT)


def _tile(n, want):
    t = min(n, want)
    while n % t:
        t //= 2
    return t


def _silu(v):
    return v * jax.nn.sigmoid(v)


def _rms_mod(x, g, shift, scale):
    ms = jnp.mean(x * x, axis=-1, keepdims=True)
    return (x * lax.rsqrt(ms + EPS) * g) * (1.0 + scale) + shift


def _adaln_kernel(c_ref, w_ref, b_ref, o_ref):
    a = _silu(c_ref[...])
    o_ref[...] = jnp.dot(a.astype(BF16), w_ref[...].astype(BF16),
                         preferred_element_type=F32) + b_ref[...]


def _adaln(cond8, w_ada, b_ada):
    L, D, D6 = w_ada.shape
    tn = _tile(D6, 1024)
    return pl.pallas_call(
        _adaln_kernel,
        out_shape=jax.ShapeDtypeStruct((L, SUBLANES, D6), F32),
        grid=(L, D6 // tn),
        in_specs=[
            pl.BlockSpec((SUBLANES, D), lambda l, j: (0, 0)),
            pl.BlockSpec((None, D, tn), lambda l, j: (l, 0, j)),
            pl.BlockSpec((None, 1, tn), lambda l, j: (l, 0, j)),
        ],
        out_specs=pl.BlockSpec((None, SUBLANES, tn), lambda l, j: (l, 0, j)),
        compiler_params=_cparams(("parallel", "parallel")),
        name="adaln",
    )(cond8, w_ada, b_ada.reshape(L, 1, D6))


def _rope(a, cos, sin_signed):
    lane = lax.broadcasted_iota(jnp.int32, a.shape, 1)
    partner = jnp.where((lane & 32) == 0, pltpu.roll(a, 96, 1), pltpu.roll(a, 32, 1))
    return a * cos + partner * sin_signed


def _qkv_kernel(*refs, n_norm_tiles, rope):
    if rope:
        x_ref, g1_ref, sh_ref, sc_ref, w_ref, gain_ref, cos_ref, sin_ref, o_ref, h_scr = refs
    else:
        x_ref, g1_ref, sh_ref, sc_ref, w_ref, gain_ref, o_ref, h_scr = refs
    j = pl.program_id(1)

    @pl.when(j == 0)
    def _():
        h_scr[...] = _rms_mod(x_ref[...], g1_ref[...], sh_ref[...], sc_ref[...]).astype(BF16)

    acc = jnp.dot(h_scr[...], w_ref[...], preferred_element_type=F32)

    @pl.when(j < n_norm_tiles)
    def _():
        for hh in range(acc.shape[1] // HEAD_DIM):
            sl = slice(hh * HEAD_DIM, (hh + 1) * HEAD_DIM)
            a = acc[:, sl]
            ms = jnp.mean(a * a, axis=-1, keepdims=True)
            a = a * lax.rsqrt(ms + EPS) * gain_ref[:, sl]
            if rope:
                a = _rope(a, cos_ref[...], sin_ref[...])
            o_ref[:, sl] = a.astype(o_ref.dtype)

    @pl.when(j >= n_norm_tiles)
    def _():
        o_ref[...] = acc.astype(o_ref.dtype)


def _qkv_proj(x2, g1, modv, layer, mod_row_fn, w_bf16, gain, n_norm_cols, rope_tabs, tm):
    N, D = x2.shape
    ncols = w_bf16.shape[1]
    tn = _tile(n_norm_cols, 512)
    assert ncols % tn == 0 and tn % HEAD_DIM == 0
    rope = rope_tabs is not None
    in_specs = [
        pl.BlockSpec((tm, D), lambda i, j: (i, 0)),
        pl.BlockSpec((1, D), lambda i, j: (0, 0)),
        pl.BlockSpec((None, None, None, 1, D), lambda i, j: (layer, 0, mod_row_fn(i), 0, 0)),
        pl.BlockSpec((None, None, None, 1, D), lambda i, j: (layer, 1, mod_row_fn(i), 0, 0)),
        pl.BlockSpec((D, tn), lambda i, j: (0, j)),
        pl.BlockSpec((1, tn), lambda i, j: (0, j)),
    ]
    args = [x2, g1.reshape(1, D), modv, modv, w_bf16, gain]
    if rope:
        cos, sin = rope_tabs
        tpb = cos.shape[0] // tm
        in_specs += [pl.BlockSpec((tm, HEAD_DIM), lambda i, j: (i % tpb, 0))] * 2
        args += [cos, sin]
    return pl.pallas_call(
        functools.partial(_qkv_kernel, n_norm_tiles=n_norm_cols // tn, rope=rope),
        out_shape=jax.ShapeDtypeStruct((N, ncols), BF16),
        grid=(N // tm, ncols // tn),
        in_specs=in_specs,
        out_specs=pl.BlockSpec((tm, tn), lambda i, j: (i, j)),
        scratch_shapes=[pltpu.VMEM((tm, D), BF16)],
        compiler_params=_cparams(("parallel", "arbitrary")),
        name="qkv_rope" if rope else "ctx_kv",
    )(*args)


def _rope_tables(S):
    rows = S // GRID_W
    row = jnp.repeat(jnp.arange(rows), GRID_W).astype(F32)
    col = jnp.tile(jnp.arange(GRID_W), rows).astype(F32)
    nf = HEAD_DIM // 4
    inv = ROPE_BASE ** (-jnp.arange(nf, dtype=F32) / nf)
    ar, ac = row[:, None] * inv, col[:, None] * inv
    cos = jnp.concatenate([jnp.cos(ar), jnp.cos(ar), jnp.cos(ac), jnp.cos(ac)], axis=1)
    sin = jnp.concatenate([-jnp.sin(ar), jnp.sin(ar), -jnp.sin(ac), jnp.sin(ac)], axis=1)
    return cos, sin


def _attn_kernel(sink_ref, bias_ref, q_ref, kp_ref, kc_ref, kn_ref, vp_ref, vc_ref, vn_ref, kx_ref,
                 vx_ref, o_ref, k_scr, v_scr, *, tq):
    kh = pl.program_id(2)
    nloc = tq + 2 * WINDOW
    nctx = kx_ref.shape[0]
    for scr, prev, cur, nxt, cx in ((k_scr, kp_ref, kc_ref, kn_ref, kx_ref),
                                    (v_scr, vp_ref, vc_ref, vn_ref, vx_ref)):
        scr[0:WINDOW, :] = prev[...]
        scr[WINDOW:WINDOW + tq, :] = cur[...]
        scr[WINDOW + tq:nloc, :] = nxt[...]
        scr[nloc:nloc + nctx, :] = cx[...]
    kall = k_scr[...]
    vall = v_scr[...]
    for g in range(GQA_GROUP):
        sl = slice(g * HEAD_DIM, (g + 1) * HEAD_DIM)
        s = lax.dot_general(q_ref[:, sl], kall, (((1,), (1,)), ((), ())),
                            preferred_element_type=F32)
        s = s + bias_ref[...]
        sink = sink_ref[kh * GQA_GROUP + g]
        m = jnp.maximum(jnp.max(s, axis=-1, keepdims=True), sink)
        p = jnp.exp(s - m)
        denom = jnp.sum(p, axis=-1, keepdims=True) + jnp.exp(sink - m)
        o = jnp.dot(p.astype(BF16), vall, preferred_element_type=F32)
        o_ref[:, sl] = (o / denom).astype(o_ref.dtype)


def _attention(qkv, kvc, sink, B, S, tq):
    N, ncols = qkv.shape
    C = kvc.shape[0] // B
    KV = kvc.shape[1] // (2 * HEAD_DIM)
    H = KV * GQA_GROUP
    assert ncols == (H + 2 * KV) * HEAD_DIM and tq % WINDOW == 0 and S % tq == 0
    tpb, wpb, wpt = S // tq, S // WINDOW, tq // WINDOW
    gw = GQA_GROUP * HEAD_DIM

    def prev_map(col0):
        return lambda b, i, k: (jnp.maximum(b * wpb + i * wpt - 1, b * wpb), col0 + k)

    def cur_map(col0):
        return lambda b, i, k: (b * tpb + i, col0 + k)

    def next_map(col0):
        return lambda b, i, k: (jnp.minimum(b * wpb + (i + 1) * wpt, (b + 1) * wpb - 1), col0 + k)

    halo = lambda m: pl.BlockSpec((WINDOW, HEAD_DIM), m)
    cur = lambda m: pl.BlockSpec((tq, HEAD_DIM), m)
    nkeys = tq + 2 * WINDOW + C
    r = jnp.arange(tq)[:, None]
    c = jnp.arange(nkeys)[None, :]
    band = jnp.abs(r - (c - WINDOW)) <= WINDOW
    variants = [band & ((c >= WINDOW) | (not first)) & ((c < tq + WINDOW) | (not last))
                for last in (False, True) for first in (False, True)]
    bias = jnp.where(jnp.stack(variants) | (c >= tq + 2 * WINDOW), 0.0, NEG).astype(F32)
    in_specs = [
        pl.BlockSpec(memory_space=pltpu.SMEM),
        pl.BlockSpec((None, tq, nkeys),
                     lambda b, i, k: ((i == 0).astype(jnp.int32)
                                      + 2 * (i == tpb - 1).astype(jnp.int32), 0, 0)),
        pl.BlockSpec((tq, gw), lambda b, i, k: (b * tpb + i, k)),
        halo(prev_map(H)), cur(cur_map(H)), halo(next_map(H)),
        halo(prev_map(H + KV)), cur(cur_map(H + KV)), halo(next_map(H + KV)),
        pl.BlockSpec((C, HEAD_DIM), lambda b, i, k: (b, k)),
        pl.BlockSpec((C, HEAD_DIM), lambda b, i, k: (b, KV + k)),
    ]
    return pl.pallas_call(
        functools.partial(_attn_kernel, tq=tq),
        out_shape=jax.ShapeDtypeStruct((N, H * HEAD_DIM), BF16),
        grid=(B, tpb, KV),
        in_specs=in_specs,
        out_specs=pl.BlockSpec((tq, gw), lambda b, i, k: (b * tpb + i, k)),
        scratch_shapes=[pltpu.VMEM((nkeys, HEAD_DIM), BF16), pltpu.VMEM((nkeys, HEAD_DIM), BF16)],
        compiler_params=_cparams(("parallel", "parallel", "parallel")),
        name="band_attn",
    )(sink, bias, qkv, qkv, qkv, qkv, qkv, qkv, qkv, kvc, kvc)


def _split_bf16(v):
    hi = v.astype(BF16)
    return hi, (v - hi.astype(F32)).astype(BF16)


def _post_mixer(x1, g2_ref, sh2_ref, sc2_ref, wr_ref, x1_ref, h2_ref, h2r_ref, lg_ref):
    x1_ref[...] = x1
    h2 = _rms_mod(x1, g2_ref[...], sh2_ref[...], sc2_ref[...])
    h2_ref[...] = h2.astype(h2_ref.dtype)
    h2r_ref[...] = h2.reshape(h2r_ref.shape)
    hi, lo = _split_bf16(h2)
    dot = functools.partial(jnp.dot, preferred_element_type=F32)
    lg_ref[...] = dot(hi, wr_ref[0]) + (dot(hi, wr_ref[1]) + dot(lo, wr_ref[0]))


def _oproj_kernel(o_ref, x_ref, wo_ref, ga1_ref, g2_ref, sh2_ref, sc2_ref, wr_ref, *out_refs):
    y = jnp.dot(o_ref[...], wo_ref[...], preferred_element_type=F32)
    _post_mixer(x_ref[...] + ga1_ref[...] * y, g2_ref, sh2_ref, sc2_ref, wr_ref, *out_refs)


def _mod_spec(layer, chunk, row_fn, D, nargs):
    if nargs == 1:
        return pl.BlockSpec((None, None, None, 1, D), lambda i: (layer, chunk, row_fn(i), 0, 0))
    return pl.BlockSpec((None, None, None, 1, D), lambda b, i: (layer, chunk, b, 0, 0))


def _post_mixer_out(N, D, tm, row_map):
    shapes = (jax.ShapeDtypeStruct((N, D), F32), jax.ShapeDtypeStruct((N, D), BF16),
              jax.ShapeDtypeStruct((N, 1, D), F32), jax.ShapeDtypeStruct((N, LANES), F32))
    specs = (pl.BlockSpec((tm, D), lambda *a: (row_map(*a), 0)),
             pl.BlockSpec((tm, D), lambda *a: (row_map(*a), 0)),
             pl.BlockSpec((tm, 1, D), lambda *a: (row_map(*a), 0, 0)),
             pl.BlockSpec((tm, LANES), lambda *a: (row_map(*a), 0)))
    return shapes, specs


def _router_weights(w_router):
    L, D, E = w_router.shape
    assert E <= LANES
    w = jnp.pad(w_router, ((0, 0), (0, 0), (0, LANES - E)))
    hi = w.astype(BF16)
    lo = (w - hi.astype(F32)).astype(BF16)
    return jnp.stack([hi, lo], axis=1)


def _oproj(o, x2, wo_bf16, g2, modv, layer, S, wr, tm):
    N, D = x2.shape
    qd = o.shape[1]
    tpb = S // tm
    row = lambda i: i // tpb
    shapes, specs = _post_mixer_out(N, D, tm, lambda i: i)
    return pl.pallas_call(
        _oproj_kernel,
        out_shape=shapes,
        grid=(N // tm,),
        in_specs=[
            pl.BlockSpec((tm, qd), lambda i: (i, 0)),
            pl.BlockSpec((tm, D), lambda i: (i, 0)),
            pl.BlockSpec((qd, D), lambda i: (0, 0)),
            _mod_spec(layer, 2, row, D, 1),
            pl.BlockSpec((1, D), lambda i: (0, 0)),
            _mod_spec(layer, 3, row, D, 1),
            _mod_spec(layer, 4, row, D, 1),
            pl.BlockSpec((None, 2, D, LANES), lambda i: (layer, 0, 0, 0)),
        ],
        out_specs=specs,
        compiler_params=_cparams(("parallel",)),
        name="oproj_norm_router",
    )(o, x2, wo_bf16, modv, g2.reshape(1, D), modv, modv, wr)


def _pool_kernel(xp_ref, xc_ref, xn_ref, g1_ref, sh1_ref, sc1_ref, wp_ref, ps_ref, ga1_ref,
                 g2_ref, sh2_ref, sc2_ref, wr_ref, x1_ref, h2_ref, h2r_ref, lg_ref, halo_scr,
                 *, ts, seq):
    i = pl.program_id(1)
    pos0 = i * ts
    norm = lambda v: _rms_mod(v, g1_ref[...], sh1_ref[...], sc1_ref[...])
    xc = xc_ref[...]
    hc = norm(xc)
    hp = jnp.where(pos0 > 0, norm(xp_ref[...]), 0.0)
    hn = jnp.where(pos0 + ts < seq, norm(xn_ref[...]), 0.0)
    halo_scr[...] = jnp.zeros_like(halo_scr)
    halo_scr[0:POOL_HALO, :] = hp
    halo_scr[POOL_HALO:2 * POOL_HALO, :] = hn
    hc16 = hc.astype(BF16)
    halo16 = halo_scr[...].astype(BF16)

    r_c = lax.broadcasted_iota(jnp.int32, (ts, ts), 0)
    c_c = lax.broadcasted_iota(jnp.int32, (ts, ts), 1)
    r_h = lax.broadcasted_iota(jnp.int32, (ts, LANES), 0)
    c_h = lax.broadcasted_iota(jnp.int32, (ts, LANES), 1)
    p_h = jnp.where(c_h < POOL_HALO, c_h - POOL_HALO, ts + c_h - POOL_HALO)
    in_halo = c_h < 2 * POOL_HALO
    pos = pos0 + lax.broadcasted_iota(jnp.int32, (ts, 1), 0)
    gw = xc.shape[1] // len(POOL_WINDOWS)
    ys = []
    for g, w in enumerate(POOL_WINDOWS):
        lo, hi = w // 2, w - w // 2
        sl = slice(g * gw, (g + 1) * gw)
        band_c = ((c_c >= r_c - lo) & (c_c < r_c + hi)).astype(BF16)
        band_h = (in_halo & (p_h >= r_h - lo) & (p_h < r_h + hi)).astype(BF16)
        tot = (jnp.dot(band_c, hc16[:, sl], preferred_element_type=F32)
               + jnp.dot(band_h, halo16[:, sl], preferred_element_type=F32))
        cnt = (jnp.minimum(pos + hi, seq) - jnp.maximum(pos - lo, 0)).astype(F32)
        d = tot / cnt - hc[:, sl]
        ys.append(jnp.dot(d.astype(BF16), wp_ref[g], preferred_element_type=F32))
    y = jnp.concatenate(ys, axis=1) * ps_ref[...]
    _post_mixer(xc + ga1_ref[...] * y, g2_ref, sh2_ref, sc2_ref, wr_ref,
                x1_ref, h2_ref, h2r_ref, lg_ref)


def _pool_mixer(x2, g1, g2, modv, layer, B, S, wp_bf16, pool_scale, wr, ts):
    N, D = x2.shape
    G, gw, _ = wp_bf16.shape
    tpb = S // ts
    hpt, hpb = ts // POOL_HALO, S // POOL_HALO
    shapes, specs = _post_mixer_out(N, D, ts, lambda b, i: b * tpb + i)
    mod = lambda chunk: _mod_spec(layer, chunk, None, D, 2)
    vec = lambda: pl.BlockSpec((1, D), lambda b, i: (0, 0))
    return pl.pallas_call(
        functools.partial(_pool_kernel, ts=ts, seq=S),
        out_shape=shapes,
        grid=(B, tpb),
        in_specs=[
            pl.BlockSpec((POOL_HALO, D), lambda b, i: (jnp.maximum(b * hpb + i * hpt - 1, b * hpb), 0)),
            pl.BlockSpec((ts, D), lambda b, i: (b * tpb + i, 0)),
            pl.BlockSpec((POOL_HALO, D),
                         lambda b, i: (jnp.minimum(b * hpb + (i + 1) * hpt, (b + 1) * hpb - 1), 0)),
            vec(), mod(0), mod(1),
            pl.BlockSpec((G, gw, gw), lambda b, i: (0, 0, 0)),
            vec(), mod(2), vec(), mod(3), mod(4),
            pl.BlockSpec((None, 2, D, LANES), lambda b, i: (layer, 0, 0, 0)),
        ],
        out_specs=specs,
        scratch_shapes=[pltpu.VMEM((LANES, D), F32)],
        compiler_params=_cparams(("parallel", "parallel")),
        name="pool_norm_router",
    )(x2, x2, x2, g1.reshape(1, D), modv, modv, wp_bf16, pool_scale.reshape(1, D), modv,
      g2.reshape(1, D), modv, modv, wr)


def _first_argmax(v, idx, big):
    m = jnp.max(v, axis=0, keepdims=True)
    first = jnp.min(jnp.where(v == m, idx, big), axis=0, keepdims=True)
    return m, first


def _route_kernel(lg_ref, b_ref, eidx_ref, gate_ref):
    E = b_ref.shape[0]
    s = jax.nn.sigmoid(lg_ref[...].T[0:E, :])
    sb = s + b_ref[...]
    tn = s.shape[1]
    per = E // N_EXPERT_GROUPS
    sub = lax.broadcasted_iota(jnp.int32, (per, tn), 0).astype(F32)
    gid = lax.broadcasted_iota(jnp.int32, (N_EXPERT_GROUPS, tn), 0).astype(F32)
    gs = jnp.zeros((N_EXPERT_GROUPS, tn), F32)
    for g in range(N_EXPERT_GROUPS):
        blk = sb[g * per:(g + 1) * per, :]
        m1, i1 = _first_argmax(blk, sub, float(per))
        m2 = jnp.max(jnp.where(sub == i1, -jnp.inf, blk), axis=0, keepdims=True)
        gs = jnp.where(gid == float(g), m1 + m2, gs)
    gsel = jnp.zeros_like(gs)
    for _ in range(TOPK_GROUPS):
        _, gi = _first_argmax(gs, gid, float(N_EXPERT_GROUPS))
        hit = gid == gi
        gsel = jnp.where(hit, 1.0, gsel)
        gs = jnp.where(hit, -jnp.inf, gs)
    cur = jnp.concatenate(
        [jnp.where(gsel[g:g + 1, :] > 0.0, sb[g * per:(g + 1) * per, :], NEG)
         for g in range(N_EXPERT_GROUPS)], axis=0)
    eid = lax.broadcasted_iota(jnp.int32, (E, tn), 0).astype(F32)
    sels = []
    for k in range(TOP_K):
        _, ei = _first_argmax(cur, eid, float(E))
        hit = eid == ei
        eidx_ref[k:k + 1, :] = ei.astype(jnp.int32)
        sels.append(jnp.sum(jnp.where(hit, s, 0.0), axis=0, keepdims=True))
        cur = jnp.where(hit, -jnp.inf, cur)
    denom = sels[0]
    for k in range(1, TOP_K):
        denom = denom + sels[k]
    for k in range(TOP_K):
        gate_ref[k:k + 1, :] = sels[k] / denom * ROUTED_SCALE


def _route(lg, b_router, tn):
    N = lg.shape[0]
    E = b_router.shape[0]
    return pl.pallas_call(
        _route_kernel,
        out_shape=(jax.ShapeDtypeStruct((TOP_K, N), jnp.int32),
                   jax.ShapeDtypeStruct((TOP_K, N), F32)),
        grid=(N // tn,),
        in_specs=[pl.BlockSpec((tn, LANES), lambda i: (i, 0)),
                  pl.BlockSpec((E, 1), lambda i: (0, 0))],
        out_specs=(pl.BlockSpec((TOP_K, tn), lambda i: (0, i)),
                   pl.BlockSpec((TOP_K, tn), lambda i: (0, i))),
        compiler_params=_cparams(("parallel",)),
        name="route_topk",
    )(lg, b_router.reshape(E, 1))


def _dispatch_plan(eidx, gates, E, tm, n_tiles_max):
    K, N = eidx.shape
    P = N * K
    flat_e = eidx.T.reshape(P)
    flat_g = gates.T.reshape(P)
    perm = jnp.argsort(flat_e, stable=True).astype(jnp.int32)
    counts = jnp.sum((flat_e[:, None] == jnp.arange(E, dtype=jnp.int32)[None, :]).astype(jnp.int32),
                     axis=0)
    tiles_e = (counts + tm - 1) // tm
    tile_end = jnp.cumsum(tiles_e)
    tile_start = tile_end - tiles_e
    n_tiles = tile_end[-1]
    cs = jnp.cumsum(counts) - counts
    tj = jnp.arange(n_tiles_max, dtype=jnp.int32)
    te = jnp.minimum(jnp.sum((tj[:, None] >= tile_end[None, :]).astype(jnp.int32), axis=1), E - 1)
    te = jnp.where(tj < n_tiles, te, te[jnp.maximum(n_tiles - 1, 0)])
    first = (tj - tile_start[te]) * tm
    nvalid = jnp.where(tj < n_tiles, jnp.clip(counts[te] - first, 0, tm), 0)
    r = jnp.arange(tm, dtype=jnp.int32)[None, :]
    valid = r < nvalid[:, None]
    pair = jnp.where(valid, perm[jnp.clip((cs[te] + first)[:, None] + r, 0, P - 1)], 0)
    gate = jnp.where(valid, flat_g[pair], 0.0)
    tok, k = pair // K, pair % K
    i32 = lambda v: v.astype(jnp.int32)
    shape3 = (n_tiles_max, 1, tm)
    return (i32(te), i32(n_tiles).reshape(1), i32(nvalid), i32(tok).reshape(shape3),
            i32(k * N + tok).reshape(shape3), gate.reshape(n_tiles_max * tm, 1))


ROW_BUFS = 3


def _moe_kernel(te_ref, nt_ref, nv_ref, src0_ref, src1_ref, srcn_ref, dstp_ref, gate_ref,
                wg_ref, wu_ref, wd_ref, h_hbm, out_hbm,
                xbuf, ybuf, wg_s, wu_s, wd_s, gsem, ssem, *, tm):
    i = pl.program_id(0)
    nt = nt_ref[0]
    slot = lax.rem(i, ROW_BUFS)
    other = lax.rem(i + 2, ROW_BUFS)
    tile_nv = lambda j: nv_ref[jnp.maximum(j, 0)]
    nvp = tile_nv(i - 1)

    def gather_row(t, r, s):
        pltpu.make_async_copy(h_hbm.at[t], xbuf.at[s, pl.ds(r, 1), :], gsem.at[s]).start()

    def scatter_row(d, r, s):
        pltpu.make_async_copy(ybuf.at[s, pl.ds(r, 1), :], out_hbm.at[d], ssem.at[s]).start()

    def issue(gather_idx, gs, scatter_idx, ss, part=0, parts=1):
        for r in range(part * tm // parts, (part + 1) * tm // parts):
            if gather_idx is not None:
                gather_row(gather_idx[0, 0, r], r, gs)
            if scatter_idx is not None:
                scatter_row(scatter_idx[0, 0, r], r, ss)

    def gather_wait(s):
        pltpu.make_async_copy(ybuf.at[s], xbuf.at[s], gsem.at[s]).wait()

    def scatter_wait(s, nv):
        p = tm
        while p >= 1:
            @pl.when((nv & p) != 0)
            def _(p=p):
                pltpu.make_async_copy(ybuf.at[s, pl.ds(0, p), :], xbuf.at[s, pl.ds(0, p), :],
                                      ssem.at[s]).wait()
            p //= 2

    @pl.when(i == 0)
    def _():
        issue(src0_ref, 0, None, None)
        issue(src1_ref, 1, None, None)

    @pl.when((i >= 1) & (i <= nt) & (nvp < tm))
    def _():
        def body(r, carry):
            scatter_row(dstp_ref[0, 0, r], r, other)
            return carry
        lax.fori_loop(0, nvp, body, 0)

    @pl.when(i < nt)
    def _():
        gather_wait(slot)

        @pl.when(i >= ROW_BUFS)
        def _():
            scatter_wait(slot, tile_nv(i - ROW_BUFS))

        @pl.when((i == 0) | (te_ref[i] != te_ref[jnp.maximum(i - 1, 0)]))
        def _():
            wg_s[...] = wg_ref[...].astype(BF16)
            wu_s[...] = wu_ref[...].astype(BF16)
            wd_s[...] = wd_ref[...].astype(BF16)

        def step(scatter_prev):
            copies = functools.partial(issue, srcn_ref, other,
                                       dstp_ref if scatter_prev else None, other, parts=3)
            x = xbuf[slot].astype(BF16)
            g = jnp.dot(x, wg_s[...], preferred_element_type=F32)
            copies(part=0)
            u = jnp.dot(x, wu_s[...], preferred_element_type=F32)
            copies(part=1)
            hid = (_silu(g) * u).astype(BF16)
            y = jnp.dot(hid, wd_s[...], preferred_element_type=F32) * gate_ref[...]
            copies(part=2)
            ybuf[slot] = y

        full_prev = (i >= 1) & (nvp == tm)

        @pl.when(full_prev)
        def _():
            step(True)

        @pl.when(jnp.logical_not(full_prev))
        def _():
            step(False)

    @pl.when(i == nt)
    def _():
        @pl.when(nvp == tm)
        def _():
            issue(None, None, dstp_ref, other)
        gather_wait(slot)
        gather_wait(lax.rem(i + 1, ROW_BUFS))
        for back in range(ROW_BUFS, 0, -1):
            @pl.when(nt >= back)
            def _(back=back):
                scatter_wait(lax.rem(nt - back + ROW_BUFS, ROW_BUFS), tile_nv(nt - back))


def _experts(h2r, plan, w_gate, w_up, w_down, layer, tm):
    te, nt, nv, src, dst, gate = plan
    N, _, D = h2r.shape
    _, E, _, F = w_gate.shape
    T = src.shape[0]
    assert tm & (tm - 1) == 0 and N >= tm and N * TOP_K >= tm
    smem_blk = lambda m: pl.BlockSpec((1, 1, tm), m, memory_space=pltpu.SMEM)
    clamp = lambda v: jnp.clip(v, 0, T - 1)
    wsel = lambda i, te, nt, nv: (layer, te[clamp(i)], 0, 0)
    grid_spec = pltpu.PrefetchScalarGridSpec(
        num_scalar_prefetch=3,
        grid=(T + 1,),
        in_specs=[
            smem_blk(lambda i, te, nt, nv: (0, 0, 0)),
            smem_blk(lambda i, te, nt, nv: (clamp(1), 0, 0)),
            smem_blk(lambda i, te, nt, nv: (clamp(i + 2), 0, 0)),
            smem_blk(lambda i, te, nt, nv: (clamp(i - 1), 0, 0)),
            pl.BlockSpec((tm, 1), lambda i, te, nt, nv: (clamp(i), 0)),
            pl.BlockSpec((None, None, D, F), wsel),
            pl.BlockSpec((None, None, D, F), wsel),
            pl.BlockSpec((None, None, F, D), wsel),
            pl.BlockSpec(memory_space=pl.ANY),
        ],
        out_specs=pl.BlockSpec(memory_space=pl.ANY),
        scratch_shapes=[
            pltpu.VMEM((ROW_BUFS, tm, D), F32),
            pltpu.VMEM((ROW_BUFS, tm, D), F32),
            pltpu.VMEM((D, F), BF16),
            pltpu.VMEM((D, F), BF16),
            pltpu.VMEM((F, D), BF16),
            pltpu.SemaphoreType.DMA((ROW_BUFS,)),
            pltpu.SemaphoreType.DMA((ROW_BUFS,)),
        ],
    )
    return pl.pallas_call(
        functools.partial(_moe_kernel, tm=tm),
        out_shape=jax.ShapeDtypeStruct((TOP_K * N, 1, D), F32),
        grid_spec=grid_spec,
        compiler_params=pltpu.CompilerParams(dimension_semantics=("arbitrary",),
                                             vmem_limit_bytes=VMEM_LIMIT_EXPERTS),
        name="moe_experts",
    )(te, nt, nv, src, src, src, dst, gate, w_gate, w_up, w_down, h2r)


def _combine_kernel(h2_ref, x1_ref, *refs):
    po_refs = refs[:TOP_K]
    wsg_ref, wsu_ref, wsd_ref, ga2_ref, o_ref, row_scr = refs[TOP_K:]
    h = h2_ref[...]
    g = jnp.dot(h, wsg_ref[...], preferred_element_type=F32)
    u = jnp.dot(h, wsu_ref[...], preferred_element_type=F32)
    acc = jnp.dot((_silu(g) * u).astype(BF16), wsd_ref[...], preferred_element_type=F32)
    for po_ref in po_refs:
        row_scr[...] = po_ref[...].reshape(row_scr.shape)
        acc = acc + row_scr[...]
    o_ref[...] = x1_ref[...] + ga2_ref[...] * acc


def _combine(h2, x1, pair_out, wsg, wsu, wsd, modv, layer, S, tm):
    N, D = x1.shape
    F = wsg.shape[1]
    tpb = S // tm
    nblk = N // tm
    slot_spec = lambda k: pl.BlockSpec((tm, 1, D), lambda i: (k * nblk + i, 0, 0))
    return pl.pallas_call(
        _combine_kernel,
        out_shape=jax.ShapeDtypeStruct((N, D), F32),
        grid=(nblk,),
        in_specs=[
            pl.BlockSpec((tm, D), lambda i: (i, 0)),
            pl.BlockSpec((tm, D), lambda i: (i, 0)),
            *[slot_spec(k) for k in range(TOP_K)],
            pl.BlockSpec((D, F), lambda i: (0, 0)),
            pl.BlockSpec((D, F), lambda i: (0, 0)),
            pl.BlockSpec((F, D), lambda i: (0, 0)),
            _mod_spec(layer, 5, lambda i: i // tpb, D, 1),
        ],
        out_specs=pl.BlockSpec((tm, D), lambda i: (i, 0)),
        scratch_shapes=[pltpu.VMEM((tm, D), F32)],
        compiler_params=_cparams(("parallel",)),
        name="shared_combine",
    )(h2, x1, *([pair_out] * TOP_K), wsg, wsu, wsd, modv)


def _moe_block(mixed, layer, S, modv, b_router, w_gate, w_up, w_down, ws_gate, ws_up, ws_down, tiles):
    x1, h2, h2r, lgt = mixed
    N, D = x1.shape
    E = b_router.shape[1]
    tm = tiles["expert"]
    eidx, gates = _route(lgt, b_router[layer], _tile(N, tiles["route"]))
    plan = _dispatch_plan(eidx, gates, E, tm, (N * TOP_K) // tm + E)
    pair_out = _experts(h2r, plan, w_gate, w_up, w_down, layer, tm)
    return _combine(h2, x1, pair_out, ws_gate[layer].astype(BF16), ws_up[layer].astype(BF16),
                    ws_down[layer].astype(BF16), modv, layer, S, _tile(S, tiles["combine"]))


DEFAULT_TILES = dict(qkv=1024, attn=512, oproj=256, pool=256, route=512, expert=256, combine=128)


def kernel(x, c, ctx, c_ctx, w_ada, b_ada, g_norm1, g_norm2, w_qkv, g_q, g_k, sink, w_o, w_pool,
           pool_scale, w_router, b_router, w_gate, w_up, w_down, ws_gate, ws_up, ws_down):
    return _forward(DEFAULT_TILES, x, c, ctx, c_ctx, w_ada, b_ada, g_norm1, g_norm2, w_qkv, g_q, g_k,
                    sink, w_o, w_pool, pool_scale, w_router, b_router, w_gate, w_up, w_down,
                    ws_gate, ws_up, ws_down)


def _forward(tiles, x, c, ctx, c_ctx, w_ada, b_ada, g_norm1, g_norm2, w_qkv, g_q, g_k, sink, w_o,
             w_pool, pool_scale, w_router, b_router, w_gate, w_up, w_down, ws_gate, ws_up, ws_down):
    B, S, D = x.shape
    C = ctx.shape[1]
    L = w_ada.shape[0]
    assert L == 2 and B + 1 <= SUBLANES, "two-layer trunk: attention layer then pooling layer"
    N = B * S
    qd = w_o.shape[1]
    kd = (w_qkv.shape[2] - qd) // 2

    cond8 = jnp.zeros((SUBLANES, D), F32).at[:B].set(c).at[B].set(c_ctx)
    mod = _adaln(cond8, w_ada, b_ada)
    modv = mod.reshape(L, SUBLANES, 6, D).transpose(0, 2, 1, 3).reshape(L, 6, SUBLANES, 1, D)
    wr = _router_weights(w_router)
    x2 = x.reshape(N, D)

    scale = HEAD_DIM ** -0.5
    wq = w_qkv[0].astype(BF16)
    ones_v = jnp.ones((kd,), F32)
    gain_lat = jnp.concatenate([jnp.tile(g_q[0] * scale, qd // HEAD_DIM),
                                jnp.tile(g_k[0], kd // HEAD_DIM), ones_v]).reshape(1, -1)
    gain_ctx = jnp.concatenate([jnp.tile(g_k[0], kd // HEAD_DIM), ones_v]).reshape(1, -1)
    tmq = _tile(S, tiles["qkv"])
    qkv = _qkv_proj(x2, g_norm1[0], modv, 0, lambda i: i // (S // tmq), wq, gain_lat, qd + kd,
                    _rope_tables(S), tmq)
    kvc = _qkv_proj(ctx.reshape(B * C, D), g_norm1[0], modv, 0, lambda i: B, wq[:, qd:], gain_ctx,
                    kd, None, C)
    o = _attention(qkv, kvc, sink[0], B, S, _tile(S, tiles["attn"]))
    mixed = _oproj(o, x2, w_o[0].astype(BF16), g_norm2[0], modv, 0, S, wr,
                   _tile(S, tiles["oproj"]))
    moe_w = (modv, b_router, w_gate, w_up, w_down, ws_gate, ws_up, ws_down, tiles)
    x2 = _moe_block(mixed, 0, S, *moe_w)

    mixed = _pool_mixer(x2, g_norm1[1], g_norm2[1], modv, 1, B, S, w_pool[0].astype(BF16),
                              pool_scale[0], wr, _tile(S, tiles["pool"]))
    x2 = _moe_block(mixed, 1, S, *moe_w)
    return x2.reshape(B, S, D)
```

```python
import functools

import jax
import jax.numpy as jnp
from jax import lax
from jax.experimental import pallas as pl
from jax.experimental.pallas import tpu as pltpu

HEAD_DIM = 128
GQA_GROUP = 4
GRID_W = 64
WINDOW = 128
ROPE_BASE = 10000.0
POOL_WINDOWS = (2, 4, 8, 16)
POOL_HALO = 8
N_EXPERT_GROUPS = 8
TOPK_GROUPS = 4
TOP_K = 8
ROUTED_SCALE = 2.5
EPS = 1e-6
NEG = -1e30

LANES = 128
SUBLANES = 8
VMEM_LIMIT = 52 * 1024 * 1024
VMEM_LIMIT_EXPERTS = 58 * 1024 * 1024
F32 = jnp.float32
BF16 = jnp.bfloat16


def _cparams(sem):
    return pltpu.CompilerParams(dimension_semantics=sem, vmem_limit_bytes=VMEM_LIMIT)


def _tile(n, want):
    t = min(n, want)
    while n % t:
        t //= 2
    return t


def _silu(v):
    return v * jax.nn.sigmoid(v)


def _rms_mod(x, g, shift, scale):
    ms = jnp.mean(x * x, axis=-1, keepdims=True)
    return (x * lax.rsqrt(ms + EPS) * g) * (1.0 + scale) + shift


def _adaln_kernel(c_ref, w_ref, b_ref, o_ref):
    a = _silu(c_ref[...])
    o_ref[...] = jnp.dot(a.astype(BF16), w_ref[...].astype(BF16),
                         preferred_element_type=F32) + b_ref[...]


def _adaln(cond8, w_ada, b_ada):
    L, D, D6 = w_ada.shape
    tn = _tile(D6, 1024)
    return pl.pallas_call(
        _adaln_kernel,
        out_shape=jax.ShapeDtypeStruct((L, SUBLANES, D6), F32),
        grid=(L, D6 // tn),
        in_specs=[
            pl.BlockSpec((SUBLANES, D), lambda l, j: (0, 0)),
            pl.BlockSpec((None, D, tn), lambda l, j: (l, 0, j)),
            pl.BlockSpec((None, 1, tn), lambda l, j: (l, 0, j)),
        ],
        out_specs=pl.BlockSpec((None, SUBLANES, tn), lambda l, j: (l, 0, j)),
        compiler_params=_cparams(("parallel", "parallel")),
        name="adaln",
    )(cond8, w_ada, b_ada.reshape(L, 1, D6))


def _rope(a, cos, sin_signed):
    lane = lax.broadcasted_iota(jnp.int32, a.shape, 1)
    partner = jnp.where((lane & 32) == 0, pltpu.roll(a, 96, 1), pltpu.roll(a, 32, 1))
    return a * cos + partner * sin_signed


def _qkv_kernel(*refs, n_norm_tiles, rope):
    if rope:
        x_ref, g1_ref, sh_ref, sc_ref, w_ref, gain_ref, cos_ref, sin_ref, o_ref, h_scr = refs
    else:
        x_ref, g1_ref, sh_ref, sc_ref, w_ref, gain_ref, o_ref, h_scr = refs
    j = pl.program_id(1)

    @pl.when(j == 0)
    def _():
        h_scr[...] = _rms_mod(x_ref[...], g1_ref[...], sh_ref[...], sc_ref[...]).astype(BF16)

    acc = jnp.dot(h_scr[...], w_ref[...], preferred_element_type=F32)

    @pl.when(j < n_norm_tiles)
    def _():
        for hh in range(acc.shape[1] // HEAD_DIM):
            sl = slice(hh * HEAD_DIM, (hh + 1) * HEAD_DIM)
            a = acc[:, sl]
            ms = jnp.mean(a * a, axis=-1, keepdims=True)
            a = a * lax.rsqrt(ms + EPS) * gain_ref[:, sl]
            if rope:
                a = _rope(a, cos_ref[...], sin_ref[...])
            o_ref[:, sl] = a.astype(o_ref.dtype)

    @pl.when(j >= n_norm_tiles)
    def _():
        o_ref[...] = acc.astype(o_ref.dtype)


def _qkv_proj(x2, g1, modv, layer, mod_row_fn, w_bf16, gain, n_norm_cols, rope_tabs, tm):
    N, D = x2.shape
    ncols = w_bf16.shape[1]
    tn = _tile(n_norm_cols, 512)
    assert ncols % tn == 0 and tn % HEAD_DIM == 0
    rope = rope_tabs is not None
    in_specs = [
        pl.BlockSpec((tm, D), lambda i, j: (i, 0)),
        pl.BlockSpec((1, D), lambda i, j: (0, 0)),
        pl.BlockSpec((None, None, None, 1, D), lambda i, j: (layer, 0, mod_row_fn(i), 0, 0)),
        pl.BlockSpec((None, None, None, 1, D), lambda i, j: (layer, 1, mod_row_fn(i), 0, 0)),
        pl.BlockSpec((D, tn), lambda i, j: (0, j)),
        pl.BlockSpec((1, tn), lambda i, j: (0, j)),
    ]
    args = [x2, g1.reshape(1, D), modv, modv, w_bf16, gain]
    if rope:
        cos, sin = rope_tabs
        tpb = cos.shape[0] // tm
        in_specs += [pl.BlockSpec((tm, HEAD_DIM), lambda i, j: (i % tpb, 0))] * 2
        args += [cos, sin]
    return pl.pallas_call(
        functools.partial(_qkv_kernel, n_norm_tiles=n_norm_cols // tn, rope=rope),
        out_shape=jax.ShapeDtypeStruct((N, ncols), BF16),
        grid=(N // tm, ncols // tn),
        in_specs=in_specs,
        out_specs=pl.BlockSpec((tm, tn), lambda i, j: (i, j)),
        scratch_shapes=[pltpu.VMEM((tm, D), BF16)],
        compiler_params=_cparams(("parallel", "arbitrary")),
        name="qkv_rope" if rope else "ctx_kv",
    )(*args)


def _rope_tables(S):
    rows = S // GRID_W
    row = jnp.repeat(jnp.arange(rows), GRID_W).astype(F32)
    col = jnp.tile(jnp.arange(GRID_W), rows).astype(F32)
    nf = HEAD_DIM // 4
    inv = ROPE_BASE ** (-jnp.arange(nf, dtype=F32) / nf)
    ar, ac = row[:, None] * inv, col[:, None] * inv
    cos = jnp.concatenate([jnp.cos(ar), jnp.cos(ar), jnp.cos(ac), jnp.cos(ac)], axis=1)
    sin = jnp.concatenate([-jnp.sin(ar), jnp.sin(ar), -jnp.sin(ac), jnp.sin(ac)], axis=1)
    return cos, sin


def _attn_kernel(sink_ref, bias_ref, q_ref, kp_ref, kc_ref, kn_ref, vp_ref, vc_ref, vn_ref, kx_ref,
                 vx_ref, o_ref, k_scr, v_scr, *, tq):
    kh = pl.program_id(2)
    nloc = tq + 2 * WINDOW
    nctx = kx_ref.shape[0]
    for scr, prev, cur, nxt, cx in ((k_scr, kp_ref, kc_ref, kn_ref, kx_ref),
                                    (v_scr, vp_ref, vc_ref, vn_ref, vx_ref)):
        scr[0:WINDOW, :] = prev[...]
        scr[WINDOW:WINDOW + tq, :] = cur[...]
        scr[WINDOW + tq:nloc, :] = nxt[...]
        scr[nloc:nloc + nctx, :] = cx[...]
    kall = k_scr[...]
    vall = v_scr[...]
    for g in range(GQA_GROUP):
        sl = slice(g * HEAD_DIM, (g + 1) * HEAD_DIM)
        s = lax.dot_general(q_ref[:, sl], kall, (((1,), (1,)), ((), ())),
                            preferred_element_type=F32)
        s = s + bias_ref[...]
        sink = sink_ref[kh * GQA_GROUP + g]
        m = jnp.maximum(jnp.max(s, axis=-1, keepdims=True), sink)
        p = jnp.exp(s - m)
        denom = jnp.sum(p, axis=-1, keepdims=True) + jnp.exp(sink - m)
        o = jnp.dot(p.astype(BF16), vall, preferred_element_type=F32)
        o_ref[:, sl] = (o / denom).astype(o_ref.dtype)


def _attention(qkv, kvc, sink, B, S, tq):
    N, ncols = qkv.shape
    C = kvc.shape[0] // B
    KV = kvc.shape[1] // (2 * HEAD_DIM)
    H = KV * GQA_GROUP
    assert ncols == (H + 2 * KV) * HEAD_DIM and tq % WINDOW == 0 and S % tq == 0
    tpb, wpb, wpt = S // tq, S // WINDOW, tq // WINDOW
    gw = GQA_GROUP * HEAD_DIM

    def prev_map(col0):
        return lambda b, i, k: (jnp.maximum(b * wpb + i * wpt - 1, b * wpb), col0 + k)

    def cur_map(col0):
        return lambda b, i, k: (b * tpb + i, col0 + k)

    def next_map(col0):
        return lambda b, i, k: (jnp.minimum(b * wpb + (i + 1) * wpt, (b + 1) * wpb - 1), col0 + k)

    halo = lambda m: pl.BlockSpec((WINDOW, HEAD_DIM), m)
    cur = lambda m: pl.BlockSpec((tq, HEAD_DIM), m)
    nkeys = tq + 2 * WINDOW + C
    r = jnp.arange(tq)[:, None]
    c = jnp.arange(nkeys)[None, :]
    band = jnp.abs(r - (c - WINDOW)) <= WINDOW
    variants = [band & ((c >= WINDOW) | (not first)) & ((c < tq + WINDOW) | (not last))
                for last in (False, True) for first in (False, True)]
    bias = jnp.where(jnp.stack(variants) | (c >= tq + 2 * WINDOW), 0.0, NEG).astype(F32)
    in_specs = [
        pl.BlockSpec(memory_space=pltpu.SMEM),
        pl.BlockSpec((None, tq, nkeys),
                     lambda b, i, k: ((i == 0).astype(jnp.int32)
                                      + 2 * (i == tpb - 1).astype(jnp.int32), 0, 0)),
        pl.BlockSpec((tq, gw), lambda b, i, k: (b * tpb + i, k)),
        halo(prev_map(H)), cur(cur_map(H)), halo(next_map(H)),
        halo(prev_map(H + KV)), cur(cur_map(H + KV)), halo(next_map(H + KV)),
        pl.BlockSpec((C, HEAD_DIM), lambda b, i, k: (b, k)),
        pl.BlockSpec((C, HEAD_DIM), lambda b, i, k: (b, KV + k)),
    ]
    return pl.pallas_call(
        functools.partial(_attn_kernel, tq=tq),
        out_shape=jax.ShapeDtypeStruct((N, H * HEAD_DIM), BF16),
        grid=(B, tpb, KV),
        in_specs=in_specs,
        out_specs=pl.BlockSpec((tq, gw), lambda b, i, k: (b * tpb + i, k)),
        scratch_shapes=[pltpu.VMEM((nkeys, HEAD_DIM), BF16), pltpu.VMEM((nkeys, HEAD_DIM), BF16)],
        compiler_params=_cparams(("parallel", "parallel", "parallel")),
        name="band_attn",
    )(sink, bias, qkv, qkv, qkv, qkv, qkv, qkv, qkv, kvc, kvc)


def _split_bf16(v):
    hi = v.astype(BF16)
    return hi, (v - hi.astype(F32)).astype(BF16)


def _post_mixer(x1, g2_ref, sh2_ref, sc2_ref, wr_ref, x1_ref, h2_ref, h2r_ref, lg_ref):
    x1_ref[...] = x1
    h2 = _rms_mod(x1, g2_ref[...], sh2_ref[...], sc2_ref[...])
    h2_ref[...] = h2.astype(h2_ref.dtype)
    h2r_ref[...] = h2.reshape(h2r_ref.shape)
    hi, lo = _split_bf16(h2)
    dot = functools.partial(jnp.dot, preferred_element_type=F32)
    lg_ref[...] = dot(hi, wr_ref[0]) + (dot(hi, wr_ref[1]) + dot(lo, wr_ref[0]))


def _oproj_kernel(o_ref, x_ref, wo_ref, ga1_ref, g2_ref, sh2_ref, sc2_ref, wr_ref, *out_refs):
    y = jnp.dot(o_ref[...], wo_ref[...], preferred_element_type=F32)
    _post_mixer(x_ref[...] + ga1_ref[...] * y, g2_ref, sh2_ref, sc2_ref, wr_ref, *out_refs)


def _mod_spec(layer, chunk, row_fn, D, nargs):
    if nargs == 1:
        return pl.BlockSpec((None, None, None, 1, D), lambda i: (layer, chunk, row_fn(i), 0, 0))
    return pl.BlockSpec((None, None, None, 1, D), lambda b, i: (layer, chunk, b, 0, 0))


def _post_mixer_out(N, D, tm, row_map):
    shapes = (jax.ShapeDtypeStruct((N, D), F32), jax.ShapeDtypeStruct((N, D), BF16),
              jax.ShapeDtypeStruct((N, 1, D), F32), jax.ShapeDtypeStruct((N, LANES), F32))
    specs = (pl.BlockSpec((tm, D), lambda *a: (row_map(*a), 0)),
             pl.BlockSpec((tm, D), lambda *a: (row_map(*a), 0)),
             pl.BlockSpec((tm, 1, D), lambda *a: (row_map(*a), 0, 0)),
             pl.BlockSpec((tm, LANES), lambda *a: (row_map(*a), 0)))
    return shapes, specs


def _router_weights(w_router):
    L, D, E = w_router.shape
    assert E <= LANES
    w = jnp.pad(w_router, ((0, 0), (0, 0), (0, LANES - E)))
    hi = w.astype(BF16)
    lo = (w - hi.astype(F32)).astype(BF16)
    return jnp.stack([hi, lo], axis=1)


def _oproj(o, x2, wo_bf16, g2, modv, layer, S, wr, tm):
    N, D = x2.shape
    qd = o.shape[1]
    tpb = S // tm
    row = lambda i: i // tpb
    shapes, specs = _post_mixer_out(N, D, tm, lambda i: i)
    return pl.pallas_call(
        _oproj_kernel,
        out_shape=shapes,
        grid=(N // tm,),
        in_specs=[
            pl.BlockSpec((tm, qd), lambda i: (i, 0)),
            pl.BlockSpec((tm, D), lambda i: (i, 0)),
            pl.BlockSpec((qd, D), lambda i: (0, 0)),
            _mod_spec(layer, 2, row, D, 1),
            pl.BlockSpec((1, D), lambda i: (0, 0)),
            _mod_spec(layer, 3, row, D, 1),
            _mod_spec(layer, 4, row, D, 1),
            pl.BlockSpec((None, 2, D, LANES), lambda i: (layer, 0, 0, 0)),
        ],
        out_specs=specs,
        compiler_params=_cparams(("parallel",)),
        name="oproj_norm_router",
    )(o, x2, wo_bf16, modv, g2.reshape(1, D), modv, modv, wr)


def _pool_kernel(xp_ref, xc_ref, xn_ref, g1_ref, sh1_ref, sc1_ref, wp_ref, ps_ref, ga1_ref,
                 g2_ref, sh2_ref, sc2_ref, wr_ref, x1_ref, h2_ref, h2r_ref, lg_ref, halo_scr,
                 *, ts, seq):
    i = pl.program_id(1)
    pos0 = i * ts
    norm = lambda v: _rms_mod(v, g1_ref[...], sh1_ref[...], sc1_ref[...])
    xc = xc_ref[...]
    hc = norm(xc)
    hp = jnp.where(pos0 > 0, norm(xp_ref[...]), 0.0)
    hn = jnp.where(pos0 + ts < seq, norm(xn_ref[...]), 0.0)
    halo_scr[...] = jnp.zeros_like(halo_scr)
    halo_scr[0:POOL_HALO, :] = hp
    halo_scr[POOL_HALO:2 * POOL_HALO, :] = hn
    hc16 = hc.astype(BF16)
    halo16 = halo_scr[...].astype(BF16)

    r_c = lax.broadcasted_iota(jnp.int32, (ts, ts), 0)
    c_c = lax.broadcasted_iota(jnp.int32, (ts, ts), 1)
    r_h = lax.broadcasted_iota(jnp.int32, (ts, LANES), 0)
    c_h = lax.broadcasted_iota(jnp.int32, (ts, LANES), 1)
    p_h = jnp.where(c_h < POOL_HALO, c_h - POOL_HALO, ts + c_h - POOL_HALO)
    in_halo = c_h < 2 * POOL_HALO
    pos = pos0 + lax.broadcasted_iota(jnp.int32, (ts, 1), 0)
    gw = xc.shape[1] // len(POOL_WINDOWS)
    ys = []
    for g, w in enumerate(POOL_WINDOWS):
        lo, hi = w // 2, w - w // 2
        sl = slice(g * gw, (g + 1) * gw)
        band_c = ((c_c >= r_c - lo) & (c_c < r_c + hi)).astype(BF16)
        band_h = (in_halo & (p_h >= r_h - lo) & (p_h < r_h + hi)).astype(BF16)
        tot = (jnp.dot(band_c, hc16[:, sl], preferred_element_type=F32)
               + jnp.dot(band_h, halo16[:, sl], preferred_element_type=F32))
        cnt = (jnp.minimum(pos + hi, seq) - jnp.maximum(pos - lo, 0)).astype(F32)
        d = tot / cnt - hc[:, sl]
        ys.append(jnp.dot(d.astype(BF16), wp_ref[g], preferred_element_type=F32))
    y = jnp.concatenate(ys, axis=1) * ps_ref[...]
    _post_mixer(xc + ga1_ref[...] * y, g2_ref, sh2_ref, sc2_ref, wr_ref,
                x1_ref, h2_ref, h2r_ref, lg_ref)


def _pool_mixer(x2, g1, g2, modv, layer, B, S, wp_bf16, pool_scale, wr, ts):
    N, D = x2.shape
    G, gw, _ = wp_bf16.shape
    tpb = S // ts
    hpt, hpb = ts // POOL_HALO, S // POOL_HALO
    shapes, specs = _post_mixer_out(N, D, ts, lambda b, i: b * tpb + i)
    mod = lambda chunk: _mod_spec(layer, chunk, None, D, 2)
    vec = lambda: pl.BlockSpec((1, D), lambda b, i: (0, 0))
    return pl.pallas_call(
        functools.partial(_pool_kernel, ts=ts, seq=S),
        out_shape=shapes,
        grid=(B, tpb),
        in_specs=[
            pl.BlockSpec((POOL_HALO, D), lambda b, i: (jnp.maximum(b * hpb + i * hpt - 1, b * hpb), 0)),
            pl.BlockSpec((ts, D), lambda b, i: (b * tpb + i, 0)),
            pl.BlockSpec((POOL_HALO, D),
                         lambda b, i: (jnp.minimum(b * hpb + (i + 1) * hpt, (b + 1) * hpb - 1), 0)),
            vec(), mod(0), mod(1),
            pl.BlockSpec((G, gw, gw), lambda b, i: (0, 0, 0)),
            vec(), mod(2), vec(), mod(3), mod(4),
            pl.BlockSpec((None, 2, D, LANES), lambda b, i: (layer, 0, 0, 0)),
        ],
        out_specs=specs,
        scratch_shapes=[pltpu.VMEM((LANES, D), F32)],
        compiler_params=_cparams(("parallel", "parallel")),
        name="pool_norm_router",
    )(x2, x2, x2, g1.reshape(1, D), modv, modv, wp_bf16, pool_scale.reshape(1, D), modv,
      g2.reshape(1, D), modv, modv, wr)


def _first_argmax(v, idx, big):
    m = jnp.max(v, axis=0, keepdims=True)
    first = jnp.min(jnp.where(v == m, idx, big), axis=0, keepdims=True)
    return m, first


def _route_kernel(lg_ref, b_ref, eidx_ref, gate_ref):
    E = b_ref.shape[0]
    s = jax.nn.sigmoid(lg_ref[...].T[0:E, :])
    sb = s + b_ref[...]
    tn = s.shape[1]
    per = E // N_EXPERT_GROUPS
    sub = lax.broadcasted_iota(jnp.int32, (per, tn), 0).astype(F32)
    gid = lax.broadcasted_iota(jnp.int32, (N_EXPERT_GROUPS, tn), 0).astype(F32)
    gs = jnp.zeros((N_EXPERT_GROUPS, tn), F32)
    for g in range(N_EXPERT_GROUPS):
        blk = sb[g * per:(g + 1) * per, :]
        m1, i1 = _first_argmax(blk, sub, float(per))
        m2 = jnp.max(jnp.where(sub == i1, -jnp.inf, blk), axis=0, keepdims=True)
        gs = jnp.where(gid == float(g), m1 + m2, gs)
    gsel = jnp.zeros_like(gs)
    for _ in range(TOPK_GROUPS):
        _, gi = _first_argmax(gs, gid, float(N_EXPERT_GROUPS))
        hit = gid == gi
        gsel = jnp.where(hit, 1.0, gsel)
        gs = jnp.where(hit, -jnp.inf, gs)
    cur = jnp.concatenate(
        [jnp.where(gsel[g:g + 1, :] > 0.0, sb[g * per:(g + 1) * per, :], NEG)
         for g in range(N_EXPERT_GROUPS)], axis=0)
    eid = lax.broadcasted_iota(jnp.int32, (E, tn), 0).astype(F32)
    sels = []
    for k in range(TOP_K):
        _, ei = _first_argmax(cur, eid, float(E))
        hit = eid == ei
        eidx_ref[k:k + 1, :] = ei.astype(jnp.int32)
        sels.append(jnp.sum(jnp.where(hit, s, 0.0), axis=0, keepdims=True))
        cur = jnp.where(hit, -jnp.inf, cur)
    denom = sels[0]
    for k in range(1, TOP_K):
        denom = denom + sels[k]
    for k in range(TOP_K):
        gate_ref[k:k + 1, :] = sels[k] / denom * ROUTED_SCALE


def _route(lg, b_router, tn):
    N = lg.shape[0]
    E = b_router.shape[0]
    return pl.pallas_call(
        _route_kernel,
        out_shape=(jax.ShapeDtypeStruct((TOP_K, N), jnp.int32),
                   jax.ShapeDtypeStruct((TOP_K, N), F32)),
        grid=(N // tn,),
        in_specs=[pl.BlockSpec((tn, LANES), lambda i: (i, 0)),
                  pl.BlockSpec((E, 1), lambda i: (0, 0))],
        out_specs=(pl.BlockSpec((TOP_K, tn), lambda i: (0, i)),
                   pl.BlockSpec((TOP_K, tn), lambda i: (0, i))),
        compiler_params=_cparams(("parallel",)),
        name="route_topk",
    )(lg, b_router.reshape(E, 1))


def _dispatch_plan(eidx, gates, E, tm, n_tiles_max):
    K, N = eidx.shape
    P = N * K
    flat_e = eidx.T.reshape(P)
    flat_g = gates.T.reshape(P)
    perm = jnp.argsort(flat_e, stable=True).astype(jnp.int32)
    counts = jnp.sum((flat_e[:, None] == jnp.arange(E, dtype=jnp.int32)[None, :]).astype(jnp.int32),
                     axis=0)
    tiles_e = (counts + tm - 1) // tm
    tile_end = jnp.cumsum(tiles_e)
    tile_start = tile_end - tiles_e
    n_tiles = tile_end[-1]
    cs = jnp.cumsum(counts) - counts
    tj = jnp.arange(n_tiles_max, dtype=jnp.int32)
    te = jnp.minimum(jnp.sum((tj[:, None] >= tile_end[None, :]).astype(jnp.int32), axis=1), E - 1)
    te = jnp.where(tj < n_tiles, te, te[jnp.maximum(n_tiles - 1, 0)])
    first = (tj - tile_start[te]) * tm
    nvalid = jnp.where(tj < n_tiles, jnp.clip(counts[te] - first, 0, tm), 0)
    r = jnp.arange(tm, dtype=jnp.int32)[None, :]
    valid = r < nvalid[:, None]
    pair = jnp.where(valid, perm[jnp.clip((cs[te] + first)[:, None] + r, 0, P - 1)], 0)
    gate = jnp.where(valid, flat_g[pair], 0.0)
    tok, k = pair // K, pair % K
    i32 = lambda v: v.astype(jnp.int32)
    shape3 = (n_tiles_max, 1, tm)
    return (i32(te), i32(n_tiles).reshape(1), i32(nvalid), i32(tok).reshape(shape3),
            i32(k * N + tok).reshape(shape3), gate.reshape(n_tiles_max * tm, 1))


ROW_BUFS = 3


def _moe_kernel(te_ref, nt_ref, nv_ref, src0_ref, src1_ref, srcn_ref, dstp_ref, gate_ref,
                wg_ref, wu_ref, wd_ref, h_hbm, out_hbm,
                xbuf, ybuf, wg_s, wu_s, wd_s, gsem, ssem, *, tm):
    i = pl.program_id(0)
    nt = nt_ref[0]
    slot = lax.rem(i, ROW_BUFS)
    other = lax.rem(i + 2, ROW_BUFS)
    tile_nv = lambda j: nv_ref[jnp.maximum(j, 0)]
    nvp = tile_nv(i - 1)

    def gather_row(t, r, s):
        pltpu.make_async_copy(h_hbm.at[t], xbuf.at[s, pl.ds(r, 1), :], gsem.at[s]).start()

    def scatter_row(d, r, s):
        pltpu.make_async_copy(ybuf.at[s, pl.ds(r, 1), :], out_hbm.at[d], ssem.at[s]).start()

    def issue(gather_idx, gs, scatter_idx, ss, part=0, parts=1):
        for r in range(part * tm // parts, (part + 1) * tm // parts):
            if gather_idx is not None:
                gather_row(gather_idx[0, 0, r], r, gs)
            if scatter_idx is not None:
                scatter_row(scatter_idx[0, 0, r], r, ss)

    def gather_wait(s):
        pltpu.make_async_copy(ybuf.at[s], xbuf.at[s], gsem.at[s]).wait()

    def scatter_wait(s, nv):
        p = tm
        while p >= 1:
            @pl.when((nv & p) != 0)
            def _(p=p):
                pltpu.make_async_copy(ybuf.at[s, pl.ds(0, p), :], xbuf.at[s, pl.ds(0, p), :],
                                      ssem.at[s]).wait()
            p //= 2

    @pl.when(i == 0)
    def _():
        issue(src0_ref, 0, None, None)
        issue(src1_ref, 1, None, None)

    @pl.when((i >= 1) & (i <= nt) & (nvp < tm))
    def _():
        def body(r, carry):
            scatter_row(dstp_ref[0, 0, r], r, other)
            return carry
        lax.fori_loop(0, nvp, body, 0)

    @pl.when(i < nt)
    def _():
        gather_wait(slot)

        @pl.when(i >= ROW_BUFS)
        def _():
            scatter_wait(slot, tile_nv(i - ROW_BUFS))

        @pl.when((i == 0) | (te_ref[i] != te_ref[jnp.maximum(i - 1, 0)]))
        def _():
            wg_s[...] = wg_ref[...].astype(BF16)
            wu_s[...] = wu_ref[...].astype(BF16)
            wd_s[...] = wd_ref[...].astype(BF16)

        def step(scatter_prev):
            copies = functools.partial(issue, srcn_ref, other,
                                       dstp_ref if scatter_prev else None, other, parts=3)
            x = xbuf[slot].astype(BF16)
            g = jnp.dot(x, wg_s[...], preferred_element_type=F32)
            copies(part=0)
            u = jnp.dot(x, wu_s[...], preferred_element_type=F32)
            copies(part=1)
            hid = (_silu(g) * u).astype(BF16)
            y = jnp.dot(hid, wd_s[...], preferred_element_type=F32) * gate_ref[...]
            copies(part=2)
            ybuf[slot] = y

        full_prev = (i >= 1) & (nvp == tm)

        @pl.when(full_prev)
        def _():
            step(True)

        @pl.when(jnp.logical_not(full_prev))
        def _():
            step(False)

    @pl.when(i == nt)
    def _():
        @pl.when(nvp == tm)
        def _():
            issue(None, None, dstp_ref, other)
        gather_wait(slot)
        gather_wait(lax.rem(i + 1, ROW_BUFS))
        for back in range(ROW_BUFS, 0, -1):
            @pl.when(nt >= back)
            def _(back=back):
                scatter_wait(lax.rem(nt - back + ROW_BUFS, ROW_BUFS), tile_nv(nt - back))


def _experts(h2r, plan, w_gate, w_up, w_down, layer, tm):
    te, nt, nv, src, dst, gate = plan
    N, _, D = h2r.shape
    _, E, _, F = w_gate.shape
    T = src.shape[0]
    assert tm & (tm - 1) == 0 and N >= tm and N * TOP_K >= tm
    smem_blk = lambda m: pl.BlockSpec((1, 1, tm), m, memory_space=pltpu.SMEM)
    clamp = lambda v: jnp.clip(v, 0, T - 1)
    wsel = lambda i, te, nt, nv: (layer, te[clamp(i)], 0, 0)
    grid_spec = pltpu.PrefetchScalarGridSpec(
        num_scalar_prefetch=3,
        grid=(T + 1,),
        in_specs=[
            smem_blk(lambda i, te, nt, nv: (0, 0, 0)),
            smem_blk(lambda i, te, nt, nv: (clamp(1), 0, 0)),
            smem_blk(lambda i, te, nt, nv: (clamp(i + 2), 0, 0)),
            smem_blk(lambda i, te, nt, nv: (clamp(i - 1), 0, 0)),
            pl.BlockSpec((tm, 1), lambda i, te, nt, nv: (clamp(i), 0)),
            pl.BlockSpec((None, None, D, F), wsel),
            pl.BlockSpec((None, None, D, F), wsel),
            pl.BlockSpec((None, None, F, D), wsel),
            pl.BlockSpec(memory_space=pl.ANY),
        ],
        out_specs=pl.BlockSpec(memory_space=pl.ANY),
        scratch_shapes=[
            pltpu.VMEM((ROW_BUFS, tm, D), F32),
            pltpu.VMEM((ROW_BUFS, tm, D), F32),
            pltpu.VMEM((D, F), BF16),
            pltpu.VMEM((D, F), BF16),
            pltpu.VMEM((F, D), BF16),
            pltpu.SemaphoreType.DMA((ROW_BUFS,)),
            pltpu.SemaphoreType.DMA((ROW_BUFS,)),
        ],
    )
    return pl.pallas_call(
        functools.partial(_moe_kernel, tm=tm),
        out_shape=jax.ShapeDtypeStruct((TOP_K * N, 1, D), F32),
        grid_spec=grid_spec,
        compiler_params=pltpu.CompilerParams(dimension_semantics=("arbitrary",),
                                             vmem_limit_bytes=VMEM_LIMIT_EXPERTS),
        name="moe_experts",
    )(te, nt, nv, src, src, src, dst, gate, w_gate, w_up, w_down, h2r)


def _combine_kernel(h2_ref, x1_ref, *refs):
    po_refs = refs[:TOP_K]
    wsg_ref, wsu_ref, wsd_ref, ga2_ref, o_ref, row_scr = refs[TOP_K:]
    h = h2_ref[...]
    g = jnp.dot(h, wsg_ref[...], preferred_element_type=F32)
    u = jnp.dot(h, wsu_ref[...], preferred_element_type=F32)
    acc = jnp.dot((_silu(g) * u).astype(BF16), wsd_ref[...], preferred_element_type=F32)
    for po_ref in po_refs:
        row_scr[...] = po_ref[...].reshape(row_scr.shape)
        acc = acc + row_scr[...]
    o_ref[...] = x1_ref[...] + ga2_ref[...] * acc


def _combine(h2, x1, pair_out, wsg, wsu, wsd, modv, layer, S, tm):
    N, D = x1.shape
    F = wsg.shape[1]
    tpb = S // tm
    nblk = N // tm
    slot_spec = lambda k: pl.BlockSpec((tm, 1, D), lambda i: (k * nblk + i, 0, 0))
    return pl.pallas_call(
        _combine_kernel,
        out_shape=jax.ShapeDtypeStruct((N, D), F32),
        grid=(nblk,),
        in_specs=[
            pl.BlockSpec((tm, D), lambda i: (i, 0)),
            pl.BlockSpec((tm, D), lambda i: (i, 0)),
            *[slot_spec(k) for k in range(TOP_K)],
            pl.BlockSpec((D, F), lambda i: (0, 0)),
            pl.BlockSpec((D, F), lambda i: (0, 0)),
            pl.BlockSpec((F, D), lambda i: (0, 0)),
            _mod_spec(layer, 5, lambda i: i // tpb, D, 1),
        ],
        out_specs=pl.BlockSpec((tm, D), lambda i: (i, 0)),
        scratch_shapes=[pltpu.VMEM((tm, D), F32)],
        compiler_params=_cparams(("parallel",)),
        name="shared_combine",
    )(h2, x1, *([pair_out] * TOP_K), wsg, wsu, wsd, modv)


def _moe_block(mixed, layer, S, modv, b_router, w_gate, w_up, w_down, ws_gate, ws_up, ws_down, tiles):
    x1, h2, h2r, lgt = mixed
    N, D = x1.shape
    E = b_router.shape[1]
    tm = tiles["expert"]
    eidx, gates = _route(lgt, b_router[layer], _tile(N, tiles["route"]))
    plan = _dispatch_plan(eidx, gates, E, tm, (N * TOP_K) // tm + E)
    pair_out = _experts(h2r, plan, w_gate, w_up, w_down, layer, tm)
    return _combine(h2, x1, pair_out, ws_gate[layer].astype(BF16), ws_up[layer].astype(BF16),
                    ws_down[layer].astype(BF16), modv, layer, S, _tile(S, tiles["combine"]))


DEFAULT_TILES = dict(qkv=1024, attn=512, oproj=256, pool=256, route=512, expert=256, combine=256)


def kernel(x, c, ctx, c_ctx, w_ada, b_ada, g_norm1, g_norm2, w_qkv, g_q, g_k, sink, w_o, w_pool,
           pool_scale, w_router, b_router, w_gate, w_up, w_down, ws_gate, ws_up, ws_down):
    return _forward(DEFAULT_TILES, x, c, ctx, c_ctx, w_ada, b_ada, g_norm1, g_norm2, w_qkv, g_q, g_k,
                    sink, w_o, w_pool, pool_scale, w_router, b_router, w_gate, w_up, w_down,
                    ws_gate, ws_up, ws_down)


def _forward(tiles, x, c, ctx, c_ctx, w_ada, b_ada, g_norm1, g_norm2, w_qkv, g_q, g_k, sink, w_o,
             w_pool, pool_scale, w_router, b_router, w_gate, w_up, w_down, ws_gate, ws_up, ws_down):
    B, S, D = x.shape
    C = ctx.shape[1]
    L = w_ada.shape[0]
    assert L == 2 and B + 1 <= SUBLANES, "two-layer trunk: attention layer then pooling layer"
    N = B * S
    qd = w_o.shape[1]
    kd = (w_qkv.shape[2] - qd) // 2

    cond8 = jnp.zeros((SUBLANES, D), F32).at[:B].set(c).at[B].set(c_ctx)
    mod = _adaln(cond8, w_ada, b_ada)
    modv = mod.reshape(L, SUBLANES, 6, D).transpose(0, 2, 1, 3).reshape(L, 6, SUBLANES, 1, D)
    wr = _router_weights(w_router)
    x2 = x.reshape(N, D)

    scale = HEAD_DIM ** -0.5
    wq = w_qkv[0].astype(BF16)
    ones_v = jnp.ones((kd,), F32)
    gain_lat = jnp.concatenate([jnp.tile(g_q[0] * scale, qd // HEAD_DIM),
                                jnp.tile(g_k[0], kd // HEAD_DIM), ones_v]).reshape(1, -1)
    gain_ctx = jnp.concatenate([jnp.tile(g_k[0], kd // HEAD_DIM), ones_v]).reshape(1, -1)
    tmq = _tile(S, tiles["qkv"])
    qkv = _qkv_proj(x2, g_norm1[0], modv, 0, lambda i: i // (S // tmq), wq, gain_lat, qd + kd,
                    _rope_tables(S), tmq)
    kvc = _qkv_proj(ctx.reshape(B * C, D), g_norm1[0], modv, 0, lambda i: B, wq[:, qd:], gain_ctx,
                    kd, None, C)
    o = _attention(qkv, kvc, sink[0], B, S, _tile(S, tiles["attn"]))
    mixed = _oproj(o, x2, w_o[0].astype(BF16), g_norm2[0], modv, 0, S, wr,
                   _tile(S, tiles["oproj"]))
    moe_w = (modv, b_router, w_gate, w_up, w_down, ws_gate, ws_up, ws_down, tiles)
    x2 = _moe_block(mixed, 0, S, *moe_w)

    mixed = _pool_mixer(x2, g_norm1[1], g_norm2[1], modv, 1, B, S, w_pool[0].astype(BF16),
                              pool_scale[0], wr, _tile(S, tiles["pool"]))
    x2 = _moe_block(mixed, 1, S, *moe_w)
    return x2.reshape(B, S, D)
```
